```python
import jax
import jax.numpy as jnp
from jax import lax
import numpy as np

D_MODEL = 4096
BATCH = 4
SEQ = 2048
DEPTH = 2
DEC_BATCH = 128
DEC_SEQ = 8
PAST_LEN = 16384
PAGE_SIZE = 128

A_HEADS = 12
A_DK = 128
A_DV = 128
A_CONV = 4
A_CHUNK = 64
B_HEADS = 12
B_Q_RANK = 768
B_KV_RANK = 256
B_NOPE = 128
B_ROPE = 64
B_V = 128
ROPE_THETA = 10000.0
Q_BLOCK = 128
C_GROUPS = 8
C_GROUP_DIM = 128
C_CHUNK = 128
D_FF = 14336
N_EXPERTS = 8
TOP_K = 2
EPS = 1e-6

A_QK_WIDTH = A_HEADS * A_DK
A_V_WIDTH = A_HEADS * A_DV
A_CONV_CH = 2 * A_QK_WIDTH + A_V_WIDTH
C_WIDTH = C_GROUPS * C_GROUP_DIM
MIX_WIDTH = A_V_WIDTH + B_HEADS * B_V + C_WIDTH
IN_WIDTH = A_CONV_CH + A_V_WIDTH + 2 * A_HEADS + B_Q_RANK + B_KV_RANK + B_ROPE + 2 * C_WIDTH
B_SCALE = (B_NOPE + B_ROPE) ** -0.5

kernel_name = 'hymba_gdn_mla_chunkmlp_adaln_decode_step'


def rmsnorm(x, w=None):
    xf = x.astype(jnp.float32)
    y = xf * lax.rsqrt(jnp.mean(xf * xf, axis=-1, keepdims=True) + EPS)
    if w is not None:
        y = y * w.astype(jnp.float32)
    return y.astype(x.dtype)


def l2norm(x):
    return x * lax.rsqrt(jnp.sum(x * x, axis=-1, keepdims=True) + EPS)


def modulate(h, shift, scale):
    return h * (1 + scale) + shift


def ada_modulation(c, w, b):
    m = (jax.nn.silu(c) @ w + b).reshape(c.shape[0], 6, 1, -1)
    return [m[:, i] for i in range(6)]


def split_in_proj(z):
    sizes = (A_CONV_CH, A_V_WIDTH, A_HEADS, A_HEADS, B_Q_RANK, B_KV_RANK, B_ROPE, C_WIDTH, C_WIDTH)
    idx = [int(i) for i in np.cumsum(sizes)[:-1]]
    return jnp.split(z, idx, axis=-1)


def rope(x, pos):
    half = B_ROPE // 2
    inv = 1.0 / (ROPE_THETA ** (jnp.arange(half, dtype=jnp.float32) / half))
    ang = pos.astype(jnp.float32)[:, None] * inv[None, :]
    cos = jnp.cos(ang)[:, None, :]
    sin = jnp.sin(ang)[:, None, :]
    xf = x.astype(jnp.float32)
    x1, x2 = xf[..., :half], xf[..., half:]
    return jnp.concatenate([x1 * cos - x2 * sin, x2 * cos + x1 * sin], axis=-1).astype(x.dtype)


def short_conv(xc, buf, w):
    t = xc.shape[1]
    full = jnp.concatenate([buf.astype(xc.dtype), xc], axis=1)
    out = full[:, 0:t] * w[0]
    for j in range(1, A_CONV):
        out = out + full[:, j:j + t] * w[j]
    return jax.nn.silu(out), full[:, t:]


def gated_delta_rule(q, k, v, g, beta, s0):
    b, t, h, dk = q.shape
    dv = v.shape[-1]
    c = min(A_CHUNK, t)
    pad = (-t) % c
    if pad:
        pw = ((0, 0), (0, pad), (0, 0), (0, 0))
        q, k, v = jnp.pad(q, pw), jnp.pad(k, pw), jnp.pad(v, pw)
        g, beta = jnp.pad(g, pw[:3]), jnp.pad(beta, pw[:3])
    n = (t + pad) // c

    def to_chunks(x):
        x = x.reshape((b, n, c, h) + x.shape[3:])
        return jnp.moveaxis(jnp.moveaxis(x, 1, 0), 2, 3)

    qc, kc, vc, bc = to_chunks(q), to_chunks(k), to_chunks(v), to_chunks(beta)
    gc = jnp.cumsum(to_chunks(g), axis=-1)
    incl = jnp.tril(jnp.ones((c, c), dtype=bool))
    strict = jnp.tril(jnp.ones((c, c), dtype=bool), -1)
    diff = gc[..., :, None] - gc[..., None, :]
    decay = jnp.where(incl, jnp.exp(jnp.where(incl, diff, 0.0)), 0.0)
    kb = kc * bc[..., None]
    a_mat = jnp.where(strict, jnp.einsum('nbhid,nbhjd->nbhij', kb, kc) * decay, 0.0) + jnp.eye(c, dtype=q.dtype)
    w = lax.linalg.triangular_solve(a_mat, kb * jnp.exp(gc)[..., None], left_side=True, lower=True, unit_diagonal=True)
    u = lax.linalg.triangular_solve(a_mat, vc * bc[..., None], left_side=True, lower=True, unit_diagonal=True)
    qk = jnp.einsum('nbhid,nbhjd->nbhij', qc, kc) * decay
    qg = qc * jnp.exp(gc)[..., None]
    kg = kc * jnp.exp(gc[..., -1:] - gc)[..., None]
    g_last = jnp.exp(gc[..., -1])

    def step(s, xs):
        w_n, u_n, qk_n, qg_n, kg_n, gl_n = xs
        v_new = u_n - jnp.einsum('bhcd,bhde->bhce', w_n, s)
        o_n = jnp.einsum('bhcd,bhde->bhce', qg_n, s) + jnp.einsum('bhij,bhje->bhie', qk_n, v_new)
        s = s * gl_n[..., None, None] + jnp.einsum('bhcd,bhce->bhde', kg_n, v_new)
        return s, o_n

    s_fin, o = lax.scan(step, s0, (w, u, qk, qg, kg, g_last))
    o = jnp.moveaxis(jnp.moveaxis(o, 3, 2), 0, 1).reshape(b, n * c, h, dv)[:, :t]
    return o, s_fin


def gdn_group(xqkv, z, a, bt, s0, buf, conv_w, a_log, dt_bias, gdn_norm):
    xc, buf_new = short_conv(xqkv, buf, conv_w)
    b, t = xc.shape[0], xc.shape[1]
    cf = xc.astype(jnp.float32)
    q = l2norm(cf[..., :A_QK_WIDTH].reshape(b, t, A_HEADS, A_DK)) * (A_DK ** -0.5)
    k = l2norm(cf[..., A_QK_WIDTH:2 * A_QK_WIDTH].reshape(b, t, A_HEADS, A_DK))
    v = cf[..., 2 * A_QK_WIDTH:].reshape(b, t, A_HEADS, A_DV)
    beta = jax.nn.sigmoid(bt.astype(jnp.float32))
    g = -jnp.exp(a_log.astype(jnp.float32)) * jax.nn.softplus(a.astype(jnp.float32) + dt_bias.astype(jnp.float32))
    o, s_new = gated_delta_rule(q, k, v, g, beta, s0.astype(jnp.float32))
    o = rmsnorm(o, gdn_norm) * jax.nn.silu(z.astype(jnp.float32).reshape(b, t, A_HEADS, A_DV))
    return o.reshape(b, t, A_V_WIDTH).astype(xqkv.dtype), s_new, buf_new


def mla_project(cq, ckv_raw, kpe_raw, pos, q_norm, kv_norm, w_uq, w_ukv):
    b, t = cq.shape[0], cq.shape[1]
    q = (rmsnorm(cq, q_norm) @ w_uq).reshape(b, t, B_HEADS, B_NOPE + B_ROPE)
    q_nope, q_pe = q[..., :B_NOPE], rope(q[..., B_NOPE:], pos)
    ckv = rmsnorm(ckv_raw, kv_norm)
    kpe = rope(kpe_raw[:, :, None, :], pos)[:, :, 0, :]
    wkv = w_ukv.reshape(B_KV_RANK, B_HEADS, B_NOPE + B_V)
    w_uk, w_uv = wkv[..., :B_NOPE], wkv[..., B_NOPE:]
    q_lat = jnp.einsum('bthd,rhd->bthr', q_nope, w_uk)
    return q_lat, q_pe, ckv, kpe, w_uv


def mla_prompt_attend(q_lat, q_pe, ckv, kpe):
    b, t, h, r = q_lat.shape
    nb = t // Q_BLOCK
    ql = jnp.swapaxes(q_lat.reshape(b, nb, Q_BLOCK, h, r), 0, 1)
    qp = jnp.swapaxes(q_pe.reshape(b, nb, Q_BLOCK, h, B_ROPE), 0, 1)
    kpos = jnp.arange(t)

    def block(args):
        qlb, qpb, i = args
        s = (jnp.einsum('bqhr,bkr->bhqk', qlb, ckv) + jnp.einsum('bqhp,bkp->bhqk', qpb, kpe)).astype(jnp.float32) * B_SCALE
        qpos = i * Q_BLOCK + jnp.arange(Q_BLOCK)
        s = jnp.where(kpos[None, :] <= qpos[:, None], s, -jnp.inf)
        p = jax.nn.softmax(s, axis=-1).astype(ckv.dtype)
        return jnp.einsum('bhqk,bkr->bqhr', p, ckv)

    o = lax.map(block, (ql, qp, jnp.arange(nb)))
    return jnp.swapaxes(o, 0, 1).reshape(b, t, h, r)


def mla_sample_attend(q_lat, q_pe, ckv_past, kpe_past, ckv_new, kpe_new):
    t = q_lat.shape[1]
    n_past = ckv_past.shape[1]
    s_past = jnp.einsum('bthr,bsr->bhts', q_lat, ckv_past) + jnp.einsum('bthp,bsp->bhts', q_pe, kpe_past)
    s_new = jnp.einsum('bthr,bsr->bhts', q_lat, ckv_new) + jnp.einsum('bthp,bsp->bhts', q_pe, kpe_new)
    causal = jnp.tril(jnp.ones((t, t), dtype=bool))
    s_new = jnp.where(causal, s_new.astype(jnp.float32) * B_SCALE, -jnp.inf)
    s = jnp.concatenate([s_past.astype(jnp.float32) * B_SCALE, s_new], axis=-1)
    p = jax.nn.softmax(s, axis=-1).astype(ckv_new.dtype)
    return (jnp.einsum('bhts,bsr->bthr', p[..., :n_past], ckv_past)
            + jnp.einsum('bhts,bsr->bthr', p[..., n_past:], ckv_new))


def chunk_mlp(u_raw, v_raw, c_vnorm, w_s, b_s):
    b, t = u_raw.shape[0], u_raw.shape[1]
    L = min(t, C_CHUNK)
    n = t // L
    u = jax.nn.gelu(u_raw, approximate=False)
    v = rmsnorm(jax.nn.gelu(v_raw, approximate=False), c_vnorm)
    mask = jnp.tril(jnp.ones((L, L), dtype=bool))
    wm = jnp.where(mask, w_s[:, :L, :L], 0.0).astype(v.dtype)
    vc = v.reshape(b, n, L, C_GROUPS, C_GROUP_DIM)
    mixed = jnp.einsum('gts,bnsgd->bntgd', wm, vc) + b_s[:, :L].T[None, None, :, :, None]
    return (u * mixed.reshape(b, t, C_WIDTH)).astype(u_raw.dtype), v


def mixing_sublayer(x, mod, pos, s0, buf, ckv_past, kpe_past, w_in, conv_w, a_log, dt_bias, gdn_norm,
                    q_norm, kv_norm, w_uq, w_ukv, c_vnorm, w_s, b_s, w_out):
    b, t = x.shape[0], x.shape[1]
    h = modulate(rmsnorm(x), mod[0], mod[1])
    xqkv, z, a, bt, cq, ckv_raw, kpe_raw, u_raw, v_raw = split_in_proj(h @ w_in)
    o_a, s_new, buf_new = gdn_group(xqkv, z, a, bt, s0, buf, conv_w, a_log, dt_bias, gdn_norm)
    q_lat, q_pe, ckv, kpe, w_uv = mla_project(cq, ckv_raw, kpe_raw, pos, q_norm, kv_norm, w_uq, w_ukv)
    if ckv_past is None:
        o_lat = mla_prompt_attend(q_lat, q_pe, ckv, kpe)
    else:
        o_lat = mla_sample_attend(q_lat, q_pe, ckv_past, kpe_past, ckv, kpe)
    o_b = jnp.einsum('bthr,rhd->bthd', o_lat, w_uv).reshape(b, t, B_HEADS * B_V).astype(x.dtype)
    o_c, v_rows = chunk_mlp(u_raw, v_raw, c_vnorm, w_s, b_s)
    y = x + mod[2] * (jnp.concatenate([o_a, o_b, o_c], axis=-1) @ w_out)
    return y, ckv, kpe, s_new, buf_new, v_rows


def swiglu(h, wg, wu, wd):
    return (jax.nn.silu(h @ wg) * (h @ wu)) @ wd


def moe_ffn(h, w_router, e_gate, e_up, e_down, m):
    logits = (h @ w_router[m]).astype(jnp.float32)
    top_v, top_i = lax.top_k(logits, TOP_K)
    top_w = jax.nn.softmax(top_v, axis=-1)
    gates = jnp.sum(jax.nn.one_hot(top_i, N_EXPERTS, dtype=jnp.float32) * top_w[..., None], axis=-2).astype(h.dtype)
    y = jnp.zeros_like(h)
    for e in range(N_EXPERTS):
        y = y + gates[..., e:e + 1] * swiglu(h, e_gate[m, e], e_up[m, e], e_down[m, e])
    return y


def channel_sublayer(x, mod, l, w_gate, w_up, w_down, w_router, e_gate, e_up, e_down):
    h = modulate(rmsnorm(x), mod[3], mod[4])
    m = l // 2
    if l % 2 == 0:
        f = swiglu(h, w_gate[m], w_up[m], w_down[m])
    else:
        f = moe_ffn(h, w_router, e_gate, e_up, e_down, m)
    return x + mod[5] * f


def setup_inputs(seed: int = 0) -> dict:
    key = jax.random.key(seed)
    ks = jax.random.split(key, 40)
    n_pages = PAST_LEN // PAGE_SIZE
    n_used = DEC_BATCH * n_pages
    n_pool = n_used + n_used // 4
    n_dense = (DEPTH + 1) // 2
    n_moe = DEPTH // 2
    D = D_MODEL

    def nrm(k, shape, scale=1.0):
        return jax.random.normal(k, shape, jnp.float32) * scale

    page_table = jax.random.permutation(ks[7], n_pool)[:n_used].reshape(DEC_BATCH, n_pages).astype(jnp.int32)
    return {
        'x_prompt': nrm(ks[0], (BATCH, SEQ, D)),
        'x_sample': nrm(ks[1], (DEC_BATCH, DEC_SEQ, D)),
        'cache_ckv': nrm(ks[2], (DEPTH, n_pool, PAGE_SIZE, B_KV_RANK)),
        'cache_kpe': nrm(ks[3], (DEPTH, n_pool, PAGE_SIZE, B_ROPE)),
        'state_gdn': nrm(ks[4], (DEPTH, DEC_BATCH, A_HEADS, A_DK, A_DV), A_DK ** -0.5),
        'state_conv': nrm(ks[5], (DEPTH, DEC_BATCH, A_CONV - 1, A_CONV_CH)),
        'page_table': page_table,
        'c_prompt': nrm(ks[8], (BATCH, D)),
        'c_sample': nrm(ks[9], (DEC_BATCH, D)),
        'w_ada': nrm(ks[10], (DEPTH, D, 6 * D), D ** -0.5),
        'b_ada': nrm(ks[11], (DEPTH, 6 * D), 0.02),
        'w_in': nrm(ks[12], (DEPTH, D, IN_WIDTH), D ** -0.5),
        'conv_w': nrm(ks[13], (DEPTH, A_CONV, A_CONV_CH), A_CONV ** -0.5),
        'a_log': jnp.log(jax.random.uniform(ks[14], (DEPTH, A_HEADS), jnp.float32, 1.0, 16.0)),
        'dt_bias': nrm(ks[15], (DEPTH, A_HEADS), 0.1),
        'gdn_norm': 1.0 + nrm(ks[16], (DEPTH, A_DV), 0.1),
        'q_norm': 1.0 + nrm(ks[17], (DEPTH, B_Q_RANK), 0.1),
        'kv_norm': 1.0 + nrm(ks[18], (DEPTH, B_KV_RANK), 0.1),
        'w_uq': nrm(ks[19], (DEPTH, B_Q_RANK, B_HEADS * (B_NOPE + B_ROPE)), B_Q_RANK ** -0.5),
        'w_ukv': nrm(ks[20], (DEPTH, B_KV_RANK, B_HEADS * (B_NOPE + B_V)), B_KV_RANK ** -0.5),
        'c_vnorm': 1.0 + nrm(ks[21], (DEPTH, C_WIDTH), 0.1),
        'w_s': nrm(ks[22], (DEPTH, C_GROUPS, C_CHUNK, C_CHUNK), C_CHUNK ** -0.5),
        'b_s': nrm(ks[23], (DEPTH, C_GROUPS, C_CHUNK), 0.1),
        'w_out': nrm(ks[24], (DEPTH, MIX_WIDTH, D), MIX_WIDTH ** -0.5),
        'w_gate': nrm(ks[25], (n_dense, D, D_FF), D ** -0.5),
        'w_up': nrm(ks[26], (n_dense, D, D_FF), D ** -0.5),
        'w_down': nrm(ks[27], (n_dense, D_FF, D), D_FF ** -0.5),
        'w_router': nrm(ks[28], (n_moe, D, N_EXPERTS), D ** -0.5),
        'e_gate': nrm(ks[29], (n_moe, N_EXPERTS, D, D_FF), D ** -0.5),
        'e_up': nrm(ks[30], (n_moe, N_EXPERTS, D, D_FF), D ** -0.5),
        'e_down': nrm(ks[31], (n_moe, N_EXPERTS, D_FF, D), D_FF ** -0.5),
        'final_norm': 1.0 + nrm(ks[32], (D,), 0.1),
    }


def reference(x_prompt, x_sample, cache_ckv, cache_kpe, state_gdn, state_conv, page_table, c_prompt, c_sample,
              w_ada, b_ada, w_in, conv_w, a_log, dt_bias, gdn_norm, q_norm, kv_norm, w_uq, w_ukv,
              c_vnorm, w_s, b_s, w_out, w_gate, w_up, w_down, w_router, e_gate, e_up, e_down, final_norm):
    bp, tp = x_prompt.shape[0], x_prompt.shape[1]
    bs, ts = x_sample.shape[0], x_sample.shape[1]
    pos_p = jnp.arange(tp, dtype=jnp.int32)
    pos_s = PAST_LEN + jnp.arange(ts, dtype=jnp.int32)
    xp, xs = x_prompt, x_sample
    p_ckv, p_kpe, p_gdn, p_conv = [], [], [], []
    s_ckv, s_kpe, s_gdn, s_conv, s_cv = [], [], [], [], []
    for l in range(DEPTH):
        lw = (w_in[l], conv_w[l], a_log[l], dt_bias[l], gdn_norm[l], q_norm[l], kv_norm[l],
              w_uq[l], w_ukv[l], c_vnorm[l], w_s[l], b_s[l], w_out[l])
        mod_p = ada_modulation(c_prompt, w_ada[l], b_ada[l])
        mod_s = ada_modulation(c_sample, w_ada[l], b_ada[l])
        s0_p = jnp.zeros((bp, A_HEADS, A_DK, A_DV), jnp.float32)
        buf_p = jnp.zeros((bp, A_CONV - 1, A_CONV_CH), x_prompt.dtype)
        xp, ckv, kpe, g_st, cbuf, _ = mixing_sublayer(xp, mod_p, pos_p, s0_p, buf_p, None, None, *lw)
        p_ckv.append(ckv)
        p_kpe.append(kpe)
        p_gdn.append(g_st)
        p_conv.append(cbuf)
        ckv_past = cache_ckv[l, page_table].reshape(bs, -1, B_KV_RANK)
        kpe_past = cache_kpe[l, page_table].reshape(bs, -1, B_ROPE)
        xs, ckv, kpe, g_st, cbuf, v_rows = mixing_sublayer(xs, mod_s, pos_s, state_gdn[l], state_conv[l],
                                                           ckv_past, kpe_past, *lw)
        s_ckv.append(ckv)
        s_kpe.append(kpe)
        s_gdn.append(g_st)
        s_conv.append(cbuf)
        s_cv.append(v_rows)
        xp = channel_sublayer(xp, mod_p, l, w_gate, w_up, w_down, w_router, e_gate, e_up, e_down)
        xs = channel_sublayer(xs, mod_s, l, w_gate, w_up, w_down, w_router, e_gate, e_up, e_down)
    return (rmsnorm(xp, final_norm), rmsnorm(xs, final_norm),
            jnp.stack(p_ckv).astype(cache_ckv.dtype), jnp.stack(p_kpe).astype(cache_kpe.dtype),
            jnp.stack(p_gdn).astype(state_gdn.dtype), jnp.stack(p_conv).astype(state_conv.dtype),
            jnp.stack(s_ckv).astype(cache_ckv.dtype), jnp.stack(s_kpe).astype(cache_kpe.dtype),
            jnp.stack(s_gdn).astype(state_gdn.dtype), jnp.stack(s_conv).astype(state_conv.dtype),
            jnp.stack(s_cv).astype(x_sample.dtype))
```

```python
import functools

import numpy as np
import jax
import jax.numpy as jnp
from jax import lax
from jax.experimental import pallas as pl
from jax.experimental.pallas import tpu as pltpu

D_MODEL = 4096
BATCH = 4
SEQ = 2048
DEPTH = 2
DEC_BATCH = 128
DEC_SEQ = 8
PAST_LEN = 16384
PAGE_SIZE = 128
A_HEADS = 12
A_DK = 128
A_DV = 128
A_CONV = 4
A_CHUNK = 64
B_HEADS = 12
B_Q_RANK = 768
B_KV_RANK = 256
B_NOPE = 128
B_ROPE = 64
B_V = 128
ROPE_THETA = 10000.0
C_GROUPS = 8
C_GROUP_DIM = 128
C_CHUNK = 128
D_FF = 14336
N_EXPERTS = 8
TOP_K = 2
EPS = 1e-6

A_QK_WIDTH = A_HEADS * A_DK
A_V_WIDTH = A_HEADS * A_DV
A_CONV_CH = 2 * A_QK_WIDTH + A_V_WIDTH
C_WIDTH = C_GROUPS * C_GROUP_DIM
MIX_WIDTH = A_V_WIDTH + B_HEADS * B_V + C_WIDTH
B_SCALE = (B_NOPE + B_ROPE) ** -0.5
N_PAGES = PAST_LEN // PAGE_SIZE

F32 = jnp.float32
BF16 = jnp.bfloat16
LANES = 128
SUBLANES = 8
VMEM_LIMIT = 56 * 1024 * 1024

M_P = BATCH * SEQ
M_S = DEC_BATCH * DEC_SEQ
M_ALL = M_P + M_S
N_COND = BATCH + DEC_BATCH
N_COND_PAD = -(-N_COND // SUBLANES) * SUBLANES

TM = 512
NT_P = M_P // TM
NT = M_ALL // TM
TE = 256
NE_P = M_P // TE
NE = M_ALL // TE
SEQ_TE = SEQ // TE
SB_TE = TE // DEC_SEQ

Z_XQKV = 0
Z_GATE = 4608
Z_U = 6144
Z_V = 7168
Z_CKV = 8192
Z_CQ = 8448
Z_KPE = 9216
Z_AB = 9344
Z_WIDTH = 9472


def _cparams(*sem):
    return pltpu.CompilerParams(dimension_semantics=sem, vmem_limit_bytes=VMEM_LIMIT)


def _silu(x):
    return x * jax.nn.sigmoid(x)


def _gelu(x):
    return 0.5 * x * (1.0 + lax.erf(x * (2.0 ** -0.5)))


def _cast_rows(dst_ref, src_ref, chunk=256):
    rows = src_ref.shape[0]
    chunk = min(chunk, rows)

    def body(i, c):
        r = pl.multiple_of(i * chunk, chunk)
        dst_ref[pl.ds(r, chunk), :] = src_ref[pl.ds(r, chunk), :].astype(dst_ref.dtype)
        return c

    lax.fori_loop(0, rows // chunk, body, 0)


def _ada_kernel(c_ref, w_ref, b_ref, o_ref):
    a = _silu(c_ref[...]).astype(BF16)
    o_ref[...] = jnp.dot(a, w_ref[...].astype(BF16), preferred_element_type=F32) + b_ref[...]


def _ada(c_all, w_ada, b_ada):
    nl = w_ada.shape[0]
    tn = 512
    n6 = 6 * D_MODEL
    return pl.pallas_call(
        _ada_kernel,
        grid=(nl, n6 // tn),
        in_specs=[
            pl.BlockSpec((N_COND_PAD, D_MODEL), lambda l, n: (0, 0)),
            pl.BlockSpec((None, D_MODEL, tn), lambda l, n: (l, 0, n)),
            pl.BlockSpec((None, 1, tn), lambda l, n: (l, 0, n)),
        ],
        out_specs=pl.BlockSpec((None, N_COND_PAD, tn), lambda l, n: (l, 0, n)),
        out_shape=jax.ShapeDtypeStruct((nl, N_COND_PAD, n6), F32),
        compiler_params=_cparams("arbitrary", "arbitrary"),
        name="ada",
    )(c_all, w_ada, b_ada.reshape(nl, 1, n6))


def _mod_specs(which):
    sp = pl.BlockSpec((1, None, 1, D_MODEL), lambda i: (jnp.minimum(i // SEQ_TE, BATCH - 1), which, 0, 0))
    ss = pl.BlockSpec((SB_TE, None, 1, D_MODEL), lambda i: (jnp.maximum(i - NE_P, 0), which, 0, 0))
    return sp, ss


def _per_group(i, fn):
    @pl.when(i < NE_P)
    def _():
        fn(0)

    @pl.when(i >= NE_P)
    def _():
        fn(1)


def _rms(x3):
    return x3 * lax.rsqrt(jnp.mean(x3 * x3, axis=-1, keepdims=True) + EPS)


def _normmod_kernel(x_ref, shp_ref, shs_ref, scp_ref, scs_ref, h_ref):
    i = pl.program_id(0)

    def run(g):
        ns = (1, SB_TE)[g]
        sh = (shp_ref, shs_ref)[g][...]
        sc = (scp_ref, scs_ref)[g][...]
        x3 = x_ref[...].reshape(ns, TE // ns, D_MODEL)
        h = _rms(x3) * (1.0 + sc) + sh
        h_ref[...] = h.reshape(TE, D_MODEL).astype(h_ref.dtype)

    _per_group(i, run)


def _normmod(x, mod_p, mod_s, i_shift, i_scale):
    shp, shs = _mod_specs(i_shift)
    scp, scs = _mod_specs(i_scale)
    row = pl.BlockSpec((TE, D_MODEL), lambda i: (i, 0))
    return pl.pallas_call(
        _normmod_kernel,
        grid=(NE,),
        in_specs=[row, shp, shs, scp, scs],
        out_specs=row,
        out_shape=jax.ShapeDtypeStruct((M_ALL, D_MODEL), BF16),
        compiler_params=_cparams("arbitrary"),
        name="normmod",
    )(x, mod_p, mod_s, mod_p, mod_s)


def _resid_kernel(*refs, n_f, mode):
    i = pl.program_id(0)
    x_ref, f0_ref = refs[0], refs[1]
    k = 2
    if n_f == 2:
        f1_ref, tw_ref = refs[2], refs[3]
        k = 4
    gp_ref, gs_ref = refs[k], refs[k + 1]
    if mode == "mod":
        shp_ref, shs_ref, scp_ref, scs_ref, y_ref, h_ref = refs[k + 2:k + 8]
    else:
        fw_ref, out_ref = refs[k + 2:k + 4]

    def run(g):
        ns = (1, SB_TE)[g]
        f = f0_ref[...]
        if n_f == 2:
            tw = tw_ref[...]
            f = f * tw[:, 0:1] + f1_ref[...] * tw[:, 1:2]
        shape3 = (ns, TE // ns, D_MODEL)
        gate = (gp_ref, gs_ref)[g][...]
        y3 = x_ref[...].reshape(shape3) + gate * f.reshape(shape3)
        if mode == "mod":
            sh = (shp_ref, shs_ref)[g][...]
            sc = (scp_ref, scs_ref)[g][...]
            y_ref[...] = y3.reshape(TE, D_MODEL)
            h_ref[...] = (_rms(y3) * (1.0 + sc) + sh).reshape(TE, D_MODEL).astype(h_ref.dtype)
        else:
            out_ref[...] = (_rms(y3) * fw_ref[...]).reshape(TE, D_MODEL)

    _per_group(i, run)


def _resid(x, fs, tw, mod_p, mod_s, i_gate, nxt=None, final_w=None):
    row = pl.BlockSpec((TE, D_MODEL), lambda i: (i, 0))
    n_f = len(fs)
    args = [x] + list(fs)
    specs = [row] * (1 + n_f)
    if n_f == 2:
        args.append(tw)
        specs.append(pl.BlockSpec((TE, LANES), lambda i: (i, 0)))
    gp, gs = _mod_specs(i_gate)
    args += [mod_p, mod_s]
    specs += [gp, gs]
    if nxt is not None:
        np_, ns_, i_shift, i_scale = nxt
        shp, shs = _mod_specs(i_shift)
        scp, scs = _mod_specs(i_scale)
        args += [np_, ns_, np_, ns_]
        specs += [shp, shs, scp, scs]
        out_specs = [row, row]
        out_shape = [jax.ShapeDtypeStruct((M_ALL, D_MODEL), F32), jax.ShapeDtypeStruct((M_ALL, D_MODEL), BF16)]
        mode = "mod"
    else:
        args.append(final_w.reshape(1, 1, D_MODEL))
        specs.append(pl.BlockSpec((1, 1, D_MODEL), lambda i: (0, 0, 0)))
        out_specs = row
        out_shape = jax.ShapeDtypeStruct((M_ALL, D_MODEL), F32)
        mode = "final"
    return pl.pallas_call(
        functools.partial(_resid_kernel, n_f=n_f, mode=mode),
        grid=(NE,),
        in_specs=specs,
        out_specs=out_specs,
        out_shape=out_shape,
        compiler_params=_cparams("arbitrary"),
        name="resid_" + mode,
    )(*args)


def _mm_kernel(x_ref, w_ref, o_ref, *scratch, cast_w):
    if cast_w:
        wb_ref, = scratch

        @pl.when(pl.program_id(1) == 0)
        def _():
            _cast_rows(wb_ref, w_ref)

        w = wb_ref[...]
    else:
        w = w_ref[...]
    o_ref[...] = jnp.dot(x_ref[...], w, preferred_element_type=F32).astype(o_ref.dtype)


def _mm(x, w, layer, tn, out_dtype=F32, name="mm"):
    m, k = x.shape
    n = w.shape[-1]
    cast_w = w.dtype != BF16
    if w.ndim == 3:
        w_spec = pl.BlockSpec((None, k, tn), lambda j, i: (layer, 0, j))
    else:
        w_spec = pl.BlockSpec((k, tn), lambda j, i: (0, j))
    scratch = [pltpu.VMEM((k, tn), BF16)] if cast_w else []
    return pl.pallas_call(
        functools.partial(_mm_kernel, cast_w=cast_w),
        grid=(pl.cdiv(n, tn), m // TM),
        in_specs=[pl.BlockSpec((TM, k), lambda j, i: (i, 0)), w_spec],
        out_specs=pl.BlockSpec((TM, tn), lambda j, i: (i, j)),
        out_shape=jax.ShapeDtypeStruct((m, n), out_dtype),
        scratch_shapes=scratch,
        compiler_params=_cparams("arbitrary", "arbitrary"),
        name=name,
    )(x, w)


def _swiglu_kernel(te_ref, nv_ref, x_ref, wg_ref, wu_ref, o_ref, wgb_ref, wub_ref):
    i = pl.program_id(1)
    prev = te_ref[jnp.maximum(i - 1, 0)]
    changed = jnp.logical_or(i == 0, te_ref[i] != prev)

    @pl.when(changed)
    def _():
        _cast_rows(wgb_ref, wg_ref)
        _cast_rows(wub_ref, wu_ref)

    @pl.when(i < nv_ref[0])
    def _():
        x = x_ref[...]
        g = jnp.dot(x, wgb_ref[...], preferred_element_type=F32)
        u = jnp.dot(x, wub_ref[...], preferred_element_type=F32)
        o_ref[...] = (_silu(g) * u).astype(o_ref.dtype)

    @pl.when(i >= nv_ref[0])
    def _():
        o_ref[...] = jnp.zeros_like(o_ref)


def _swiglu_up(x, w_gate, w_up, layer, tile_e, n_valid, tm, tn=512):
    m = x.shape[0]
    w_spec = pl.BlockSpec((None, None, D_MODEL, tn), lambda j, i, te, nv: (layer, te[i], 0, j))
    grid_spec = pltpu.PrefetchScalarGridSpec(
        num_scalar_prefetch=2,
        grid=(D_FF // tn, m // tm),
        in_specs=[
            pl.BlockSpec((tm, D_MODEL), lambda j, i, te, nv: (jnp.minimum(i, nv[0] - 1), 0)),
            w_spec, w_spec,
        ],
        out_specs=pl.BlockSpec((tm, tn), lambda j, i, te, nv: (i, j)),
        scratch_shapes=[pltpu.VMEM((D_MODEL, tn), BF16), pltpu.VMEM((D_MODEL, tn), BF16)],
    )
    return pl.pallas_call(
        _swiglu_kernel,
        grid_spec=grid_spec,
        out_shape=jax.ShapeDtypeStruct((m, D_FF), BF16),
        compiler_params=_cparams("arbitrary", "arbitrary"),
        name="swiglu_up",
    )(tile_e, n_valid, x, w_gate, w_up)


def _down_kernel(te_ref, nv_ref, x_ref, w_ref, o_ref, acc_ref):
    i = pl.program_id(0)
    kk = pl.program_id(2)
    last = kk == pl.num_programs(2) - 1

    @pl.when(i < nv_ref[0])
    def _():
        p = jnp.dot(x_ref[...], w_ref[...].astype(BF16), preferred_element_type=F32)

        @pl.when(kk == 0)
        def _():
            acc_ref[...] = p

        @pl.when(kk > 0)
        def _():
            acc_ref[...] += p

        @pl.when(last)
        def _():
            o_ref[...] = acc_ref[...]

    @pl.when(jnp.logical_and(i >= nv_ref[0], last))
    def _():
        o_ref[...] = jnp.zeros_like(o_ref)


def _swiglu_down(hid, w_down, layer, tile_e, n_valid, tm, tn=1024, tk=512):
    m = hid.shape[0]
    grid_spec = pltpu.PrefetchScalarGridSpec(
        num_scalar_prefetch=2,
        grid=(m // tm, D_MODEL // tn, D_FF // tk),
        in_specs=[
            pl.BlockSpec((tm, tk), lambda i, j, k, te, nv: (jnp.minimum(i, nv[0] - 1), k)),
            pl.BlockSpec((None, None, tk, tn), lambda i, j, k, te, nv: (layer, te[i], k, j)),
        ],
        out_specs=pl.BlockSpec((tm, tn), lambda i, j, k, te, nv: (i, j)),
        scratch_shapes=[pltpu.VMEM((tm, tn), F32)],
    )
    return pl.pallas_call(
        _down_kernel,
        grid_spec=grid_spec,
        out_shape=jax.ShapeDtypeStruct((m, D_MODEL), F32),
        compiler_params=_cparams("arbitrary", "arbitrary", "arbitrary"),
        name="swiglu_down",
    )(tile_e, n_valid, hid, w_down)


def _router_kernel(h_ref, w_ref, ti_ref, tw_ref):
    logits = jnp.dot(h_ref[...], w_ref[...], preferred_element_type=F32)
    lane = lax.broadcasted_iota(jnp.int32, logits.shape, 1)
    neg = jnp.float32(-jnp.inf)
    logits = jnp.where(lane < N_EXPERTS, logits, neg)
    m1 = jnp.max(logits, axis=-1, keepdims=True)
    i1 = jnp.min(jnp.where(logits == m1, lane, LANES), axis=-1, keepdims=True)
    rest = jnp.where(lane == i1, neg, logits)
    m2 = jnp.max(rest, axis=-1, keepdims=True)
    i2 = jnp.min(jnp.where(rest == m2, lane, LANES), axis=-1, keepdims=True)
    e2 = jnp.exp(m2 - m1)
    den = 1.0 + e2
    ti_ref[...] = jnp.where(lane == 0, i1, jnp.where(lane == 1, i2, 0))
    tw_ref[...] = jnp.where(lane == 0, 1.0 / den, jnp.where(lane == 1, e2 / den, 0.0))


def _router(h, w_router_pad):
    row = pl.BlockSpec((TM, LANES), lambda i: (i, 0))
    return pl.pallas_call(
        _router_kernel,
        grid=(NT,),
        in_specs=[pl.BlockSpec((TM, D_MODEL), lambda i: (i, 0)), pl.BlockSpec((D_MODEL, LANES), lambda i: (0, 0))],
        out_specs=[row, row],
        out_shape=[jax.ShapeDtypeStruct((M_ALL, LANES), jnp.int32), jax.ShapeDtypeStruct((M_ALL, LANES), F32)],
        compiler_params=_cparams("arbitrary"),
        name="router",
    )(h, w_router_pad)


DENSE_DOWN_TM = 1024
assert M_ALL % DENSE_DOWN_TM == 0
MOE_TM = 512
MOE_ROWS = M_ALL * TOP_K + N_EXPERTS * MOE_TM
MOE_TILES = MOE_ROWS // MOE_TM


def _route_plan(ti):
    e_flat = ti.reshape(-1)
    onehot = (e_flat[:, None] == jnp.arange(N_EXPERTS, dtype=jnp.int32)[None, :]).astype(jnp.int32)
    counts = jnp.sum(onehot, axis=0)
    rank = jnp.sum((jnp.cumsum(onehot, axis=0) - onehot) * onehot, axis=1)
    padded = ((counts + MOE_TM - 1) // MOE_TM) * MOE_TM
    ends = jnp.cumsum(padded)
    starts = ends - padded
    pos = starts[e_flat] + rank
    src = jnp.zeros((MOE_ROWS,), jnp.int32).at[pos].set(jnp.arange(e_flat.shape[0], dtype=jnp.int32) // TOP_K)
    n_valid = ends[-1] // MOE_TM
    tile_start = jnp.arange(MOE_TILES, dtype=jnp.int32) * MOE_TM
    tile_e = jnp.sum((tile_start[:, None] >= ends[None, :]).astype(jnp.int32), axis=1)
    last_e = tile_e[jnp.maximum(n_valid - 1, 0)]
    tile_e = jnp.where(jnp.arange(MOE_TILES) < n_valid, jnp.minimum(tile_e, N_EXPERTS - 1), last_e)
    return src, pos.reshape(-1, TOP_K), tile_e.astype(jnp.int32), n_valid.reshape(1).astype(jnp.int32)


def _conv_kernel(x_ref, hp_ref, hs_ref, w_ref, o_ref, sp_ref, ss_ref):
    i = pl.program_id(0)
    part = pl.program_id(1)
    width = x_ref.shape[1]
    w = w_ref[...]

    def finish(y):
        y = _silu(y)
        qk = part < 2
        post = jnp.where(part == 0, A_DK ** -0.5, 1.0)
        for h in range(width // A_DK):
            seg = y[:, h * A_DK:(h + 1) * A_DK]
            ss = jnp.sum(seg * seg, axis=-1, keepdims=True)
            scale = jnp.where(qk, lax.rsqrt(ss + EPS), 1.0) * post
            o_ref[:, h * A_DK:(h + 1) * A_DK] = seg * scale

    @pl.when(i < NE_P)
    def _():
        first = (i % SEQ_TE) == 0
        sp_ref[0:SUBLANES, :] = jnp.where(first, 0.0, hp_ref[...])
        sp_ref[SUBLANES:, :] = x_ref[...]
        acc = sp_ref[SUBLANES - 3:SUBLANES - 3 + TE, :] * w[0:1, :]
        for j in range(1, A_CONV):
            acc = acc + sp_ref[SUBLANES - 3 + j:SUBLANES - 3 + j + TE, :] * w[j:j + 1, :]
        finish(acc)

    @pl.when(i >= NE_P)
    def _():
        ss_ref[:, 0:SUBLANES, :] = hs_ref[...].reshape(SB_TE, SUBLANES, width)
        ss_ref[:, SUBLANES:, :] = x_ref[...].reshape(SB_TE, DEC_SEQ, width)
        acc = ss_ref[:, SUBLANES - 3:SUBLANES - 3 + DEC_SEQ, :] * w[0:1, :]
        for j in range(1, A_CONV):
            acc = acc + ss_ref[:, SUBLANES - 3 + j:SUBLANES - 3 + j + DEC_SEQ, :] * w[j:j + 1, :]
        finish(acc.reshape(TE, width))


def _conv_qkv(z, conv_state_pad, conv_w_l):
    width = A_QK_WIDTH
    assert DEC_SEQ == SUBLANES
    return pl.pallas_call(
        _conv_kernel,
        grid=(NE, 3),
        in_specs=[
            pl.BlockSpec((TE, width), lambda i, p: (i, p)),
            pl.BlockSpec((SUBLANES, width), lambda i, p: (jnp.maximum(i * (TE // SUBLANES) - 1, 0), p)),
            pl.BlockSpec((TE, width), lambda i, p: (jnp.maximum(i - NE_P, 0), p)),
            pl.BlockSpec((A_CONV, width), lambda i, p: (0, p)),
        ],
        out_specs=pl.BlockSpec((TE, width), lambda i, p: (i, p)),
        out_shape=jax.ShapeDtypeStruct((M_ALL, A_CONV_CH), F32),
        scratch_shapes=[pltpu.VMEM((TE + SUBLANES, width), F32), pltpu.VMEM((SB_TE, 2 * SUBLANES, width), F32)],
        compiler_params=_cparams("arbitrary", "arbitrary"),
        name="conv_qkv",
    )(z, z, conv_state_pad, conv_w_l)


def _dot(a, b, hi):
    if hi:
        return jnp.dot(a, b, preferred_element_type=F32, precision=lax.Precision.HIGHEST)
    return jnp.dot(a.astype(BF16), b.astype(BF16), preferred_element_type=F32)


def _dot_nt(a, b, hi):
    dn = (((1,), (1,)), ((), ()))
    if hi:
        return lax.dot_general(a, b, dn, preferred_element_type=F32, precision=lax.Precision.HIGHEST)
    return lax.dot_general(a.astype(BF16), b.astype(BF16), dn, preferred_element_type=F32)


def _dot_tn(a, b, hi):
    dn = (((0,), (0,)), ((), ()))
    if hi:
        return lax.dot_general(a, b, dn, preferred_element_type=F32, precision=lax.Precision.HIGHEST)
    return lax.dot_general(a.astype(BF16), b.astype(BF16), dn, preferred_element_type=F32)


def _gdn_kernel(*refs, c, hb, has_s0, hi):
    if has_s0:
        q_ref, k_ref, v_ref, zg_ref, ab_ref, al_ref, dt_ref, nw_ref, s0_ref, o_ref, sf_ref, s_ref = refs
    else:
        q_ref, k_ref, v_ref, zg_ref, ab_ref, al_ref, dt_ref, nw_ref, o_ref, sf_ref, s_ref = refs
    hblk = pl.program_id(1)
    ch = pl.program_id(2)

    @pl.when(ch == 0)
    def _():
        if has_s0:
            s_ref[...] = s0_ref[...]
        else:
            s_ref[...] = jnp.zeros_like(s_ref)

    ab = ab_ref[...]
    x = ab + dt_ref[...]
    softplus = jnp.maximum(x, 0.0) + jnp.log1p(jnp.exp(-jnp.abs(x)))
    g_all = -jnp.exp(al_ref[...]) * softplus
    beta_all = jax.nn.sigmoid(ab)
    r_i = lax.broadcasted_iota(jnp.int32, (c, c), 0)
    c_i = lax.broadcasted_iota(jnp.int32, (c, c), 1)
    incl = r_i >= c_i
    strict = r_i > c_i
    eye = (r_i == c_i).astype(F32)
    gc_all = _dot(incl.astype(F32), g_all, True)
    sel = (lax.broadcasted_iota(jnp.int32, (2 * SUBLANES, LANES), 0)
           == lax.broadcasted_iota(jnp.int32, (2 * SUBLANES, LANES), 1)).astype(F32)
    gct_all = _dot_nt(sel, gc_all, True)
    lane = lax.broadcasted_iota(jnp.int32, (c, LANES), 1)
    sub = lax.broadcasted_iota(jnp.int32, (2 * SUBLANES, c), 0)
    nw = nw_ref[...]

    for j in range(hb):
        head = hblk * hb + j
        cols = slice(j * A_DK, (j + 1) * A_DK)
        q = q_ref[:, cols]
        k = k_ref[:, cols]
        v = v_ref[:, cols]
        gc = jnp.sum(jnp.where(lane == head, gc_all, 0.0), axis=1, keepdims=True)
        beta = jnp.sum(jnp.where(lane == head + A_HEADS, beta_all, 0.0), axis=1, keepdims=True)
        gc_row = jnp.sum(jnp.where(sub == head, gct_all, 0.0), axis=0, keepdims=True)
        gc_last = gc[c - 1:c, :]
        decay = jnp.where(incl, jnp.exp(jnp.where(incl, gc - gc_row, 0.0)), 0.0)
        kb = k * beta
        a_mat = jnp.where(strict, _dot_nt(kb, k, hi) * decay, 0.0)
        x_pow = -a_mat
        t_inv = eye + x_pow
        n_sq = max(int(np.ceil(np.log2(c))) - 1, 0)
        for _ in range(n_sq):
            x_pow = _dot(x_pow, x_pow, True)
            t_inv = t_inv + _dot(t_inv, x_pow, True)
        eg = jnp.exp(gc)
        wu = _dot(t_inv, jnp.concatenate([kb * eg, v * beta], axis=1), hi)
        w_m, u_m = wu[:, :A_DK], wu[:, A_DK:]
        qk = _dot_nt(q, k, hi) * decay
        s = s_ref[j]
        v_new = u_m - _dot(w_m, s, hi)
        o = _dot(q * eg, s, hi) + _dot(qk, v_new, hi)
        kg = k * jnp.exp(gc_last - gc)
        s_ref[j] = s * jnp.exp(gc_last) + _dot_tn(kg, v_new, hi)
        o = o * lax.rsqrt(jnp.mean(o * o, axis=-1, keepdims=True) + EPS) * nw
        o_ref[:, cols] = o * _silu(zg_ref[:, cols])

    @pl.when(ch == pl.num_programs(2) - 1)
    def _():
        sf_ref[...] = s_ref[...]


def _gdn(qkv, z, gate_rows, s0, *, row0, nseq, seq_len, c, hb, hi, name, s0_seq0=0):
    al_row, dt_row, nw_row = gate_rows
    nch = seq_len // c
    wblk = hb * A_DK
    rb0 = row0 // c
    nq = A_QK_WIDTH // wblk

    def rows(b, h, n):
        return rb0 + b * nch + n

    in_specs = [
        pl.BlockSpec((c, wblk), lambda b, h, n: (rows(b, h, n), h)),
        pl.BlockSpec((c, wblk), lambda b, h, n: (rows(b, h, n), nq + h)),
        pl.BlockSpec((c, wblk), lambda b, h, n: (rows(b, h, n), 2 * nq + h)),
        pl.BlockSpec((c, wblk), lambda b, h, n: (rows(b, h, n), Z_GATE // wblk + h)),
        pl.BlockSpec((c, LANES), lambda b, h, n: (rows(b, h, n), Z_AB // LANES)),
        pl.BlockSpec((1, LANES), lambda b, h, n: (0, 0)),
        pl.BlockSpec((1, LANES), lambda b, h, n: (0, 0)),
        pl.BlockSpec((1, LANES), lambda b, h, n: (0, 0)),
    ]
    args = [qkv, qkv, qkv, z, z, al_row, dt_row, nw_row]
    st_spec = pl.BlockSpec((None, hb, A_DK, A_DV), lambda b, h, n: (b, h, 0, 0))
    if s0 is not None:
        in_specs.append(pl.BlockSpec((None, hb, A_DK, A_DV), lambda b, h, n: (s0_seq0 + b, h, 0, 0)))
        args.append(s0)
    return pl.pallas_call(
        functools.partial(_gdn_kernel, c=c, hb=hb, has_s0=s0 is not None, hi=hi),
        grid=(nseq, A_HEADS // hb, nch),
        in_specs=in_specs,
        out_specs=[pl.BlockSpec((c, wblk), lambda b, h, n: (b * nch + n, h)), st_spec],
        out_shape=[jax.ShapeDtypeStruct((nseq * seq_len, A_V_WIDTH), F32),
                   jax.ShapeDtypeStruct((nseq, A_HEADS, A_DK, A_DV), F32)],
        scratch_shapes=[pltpu.VMEM((hb, A_DK, A_DV), F32)],
        compiler_params=_cparams("arbitrary", "arbitrary", "arbitrary"),
        name=name,
    )(*args)


ROPE_W = B_HEADS * B_ROPE
NOPE_W = B_HEADS * B_NOPE


def _rope_lanes(x, cos, sin_signed):
    w = x.shape[-1]
    half = B_ROPE // 2
    lane = lax.broadcasted_iota(jnp.int32, x.shape, 1)
    partner = jnp.where((lane % B_ROPE) < half, pltpu.roll(x, w - half, 1), pltpu.roll(x, half, 1))
    return x * cos + partner * sin_signed


def _mla_proj_kernel(cq_ref, ckv_ref, kpe_ref, cos_ref, sin_ref, qn_ref, kvn_ref, wq_ref, wkv_ref,
                     ql_ref, qp_ref, ckvo_ref, kpeo_ref, wqb_ref, wkvb_ref):
    @pl.when(pl.program_id(0) == 0)
    def _():
        _cast_rows(wqb_ref, wq_ref)
        _cast_rows(wkvb_ref, wkv_ref)

    cq = cq_ref[...]
    cqn = cq * lax.rsqrt(jnp.mean(cq * cq, axis=-1, keepdims=True) + EPS) * qn_ref[...]
    q = jnp.dot(cqn.astype(BF16), wqb_ref[...], preferred_element_type=F32)
    cos = cos_ref[...]
    sin = sin_ref[...]
    q_pe = _rope_lanes(q[:, NOPE_W:], cos, sin)
    for h in range(B_HEADS):
        q_nope = q[:, h * B_NOPE:(h + 1) * B_NOPE].astype(BF16)
        w_uk = wkvb_ref[:, h * (B_NOPE + B_V):h * (B_NOPE + B_V) + B_NOPE]
        ql_ref[h] = lax.dot_general(q_nope, w_uk, (((1,), (1,)), ((), ())),
                                    preferred_element_type=F32).astype(ql_ref.dtype)
        qp_ref[h] = q_pe[:, h * B_ROPE:(h + 1) * B_ROPE].astype(qp_ref.dtype)
    ckv = ckv_ref[...]
    ckvo_ref[...] = ckv * lax.rsqrt(jnp.mean(ckv * ckv, axis=-1, keepdims=True) + EPS) * kvn_ref[...]
    kpe = _rope_lanes(kpe_ref[...], cos[:, :LANES], sin[:, :LANES])
    kpeo_ref[...] = kpe[:, :B_ROPE]


def _mla_proj(z, cos_t, sin_t, q_norm_l, kv_norm_l, w_uq_perm, w_ukv_l):
    full = lambda shape: pl.BlockSpec(shape, lambda i: (0,) * len(shape))
    return pl.pallas_call(
        _mla_proj_kernel,
        grid=(NT,),
        in_specs=[
            pl.BlockSpec((TM, B_Q_RANK), lambda i: (i, Z_CQ // B_Q_RANK)),
            pl.BlockSpec((TM, B_KV_RANK), lambda i: (i, Z_CKV // B_KV_RANK)),
            pl.BlockSpec((TM, LANES), lambda i: (i, Z_KPE // LANES)),
            pl.BlockSpec((TM, ROPE_W), lambda i: (i, 0)),
            pl.BlockSpec((TM, ROPE_W), lambda i: (i, 0)),
            full((1, B_Q_RANK)), full((1, B_KV_RANK)),
            full((B_Q_RANK, NOPE_W + ROPE_W)), full((B_KV_RANK, B_HEADS * (B_NOPE + B_V))),
        ],
        out_specs=[
            pl.BlockSpec((B_HEADS, TM, B_KV_RANK), lambda i: (0, i, 0)),
            pl.BlockSpec((B_HEADS, TM, B_ROPE), lambda i: (0, i, 0)),
            pl.BlockSpec((TM, B_KV_RANK), lambda i: (i, 0)),
            pl.BlockSpec((TM, B_ROPE), lambda i: (i, 0)),
        ],
        out_shape=[
            jax.ShapeDtypeStruct((B_HEADS, M_ALL, B_KV_RANK), F32),
            jax.ShapeDtypeStruct((B_HEADS, M_ALL, B_ROPE), F32),
            jax.ShapeDtypeStruct((M_ALL, B_KV_RANK), F32),
            jax.ShapeDtypeStruct((M_ALL, B_ROPE), F32),
        ],
        scratch_shapes=[pltpu.VMEM((B_Q_RANK, NOPE_W + ROPE_W), BF16),
                        pltpu.VMEM((B_KV_RANK, B_HEADS * (B_NOPE + B_V)), BF16)],
        compiler_params=_cparams("arbitrary"),
        name="mla_proj",
    )(z, z, z, cos_t, sin_t, q_norm_l.reshape(1, -1), kv_norm_l.reshape(1, -1), w_uq_perm, w_ukv_l)


def _uv_project(acc, l, wkv_ref, o_ref, rows):
    o_lat = acc / l
    for h in range(B_HEADS):
        lo = h * (B_NOPE + B_V) + B_NOPE
        w_uv = wkv_ref[:, lo:lo + B_V].astype(BF16)
        o_h = o_lat[h * rows:(h + 1) * rows, :].astype(BF16)
        o_ref[:, h * B_V:(h + 1) * B_V] = jnp.dot(o_h, w_uv, preferred_element_type=F32)


PA_TQ = 128
PA_TK = 512


def _pattn_kernel(ql_ref, qp_ref, ckv_ref, kpe_ref, wkv_ref, o_ref, m_ref, l_ref, acc_ref):
    qb = pl.program_id(1)
    kb = pl.program_id(2)
    k_last = (qb * PA_TQ + PA_TQ - 1) // PA_TK
    rows = B_HEADS * PA_TQ

    @pl.when(kb == 0)
    def _():
        m_ref[...] = jnp.full_like(m_ref, -jnp.inf)
        l_ref[...] = jnp.zeros_like(l_ref)
        acc_ref[...] = jnp.zeros_like(acc_ref)

    @pl.when(kb <= k_last)
    def _():
        ql = ql_ref[...].reshape(rows, B_KV_RANK).astype(BF16)
        qp = qp_ref[...].reshape(rows, B_ROPE).astype(BF16)
        ckv = ckv_ref[...].astype(BF16)
        kpe = kpe_ref[...].astype(BF16)
        dn = (((1,), (1,)), ((), ()))
        s = (lax.dot_general(ql, ckv, dn, preferred_element_type=F32)
             + lax.dot_general(qp, kpe, dn, preferred_element_type=F32)) * B_SCALE
        qpos = qb * PA_TQ + lax.broadcasted_iota(jnp.int32, (B_HEADS, PA_TQ, PA_TK), 1).reshape(rows, PA_TK)
        kpos = kb * PA_TK + lax.broadcasted_iota(jnp.int32, (rows, PA_TK), 1)
        s = jnp.where(kpos <= qpos, s, -jnp.inf)
        m_old = m_ref[...]
        m_new = jnp.maximum(m_old, jnp.max(s, axis=-1, keepdims=True))
        alpha = jnp.exp(m_old - m_new)
        p = jnp.exp(s - m_new)
        l_ref[...] = alpha * l_ref[...] + jnp.sum(p, axis=-1, keepdims=True)
        acc_ref[...] = alpha * acc_ref[...] + jnp.dot(p.astype(BF16), ckv, preferred_element_type=F32)
        m_ref[...] = m_new

    @pl.when(kb == k_last)
    def _():
        _uv_project(acc_ref[...], l_ref[...], wkv_ref, o_ref, PA_TQ)


def _prompt_attn(q_lat, q_pe, ckv, kpe, w_ukv_l):
    nqb = SEQ // PA_TQ
    nkb = SEQ // PA_TK
    rows = B_HEADS * PA_TQ

    def kv_idx(b, qb, kb):
        return (b * nkb + jnp.minimum(kb, (qb * PA_TQ + PA_TQ - 1) // PA_TK), 0)

    return pl.pallas_call(
        _pattn_kernel,
        grid=(BATCH, nqb, nkb),
        in_specs=[
            pl.BlockSpec((B_HEADS, PA_TQ, B_KV_RANK), lambda b, qb, kb: (0, b * nqb + qb, 0)),
            pl.BlockSpec((B_HEADS, PA_TQ, B_ROPE), lambda b, qb, kb: (0, b * nqb + qb, 0)),
            pl.BlockSpec((PA_TK, B_KV_RANK), kv_idx),
            pl.BlockSpec((PA_TK, B_ROPE), kv_idx),
            pl.BlockSpec((B_KV_RANK, B_HEADS * (B_NOPE + B_V)), lambda b, qb, kb: (0, 0)),
        ],
        out_specs=pl.BlockSpec((PA_TQ, B_HEADS * B_V), lambda b, qb, kb: (b * nqb + qb, 0)),
        out_shape=jax.ShapeDtypeStruct((M_P, B_HEADS * B_V), F32),
        scratch_shapes=[pltpu.VMEM((rows, 1), F32), pltpu.VMEM((rows, 1), F32), pltpu.VMEM((rows, B_KV_RANK), F32)],
        compiler_params=_cparams("arbitrary", "arbitrary", "arbitrary"),
        name="prompt_attn",
    )(q_lat, q_pe, ckv, kpe, w_ukv_l)


SA_PG = 16
SA_NCH = N_PAGES // SA_PG
SA_ROWS = B_HEADS * DEC_SEQ


def _sattn_kernel(pt_ref, *refs):
    ql_ref, qp_ref, cn_ref, kn_ref, wkv_ref = refs[:5]
    ckv_refs = refs[5:5 + SA_PG]
    kpe_refs = refs[5 + SA_PG:5 + 2 * SA_PG]
    o_ref, m_ref, l_ref, acc_ref, s_ref = refs[5 + 2 * SA_PG:]
    ch = pl.program_id(1)
    dn = (((1,), (1,)), ((), ()))
    ql = ql_ref[...].reshape(SA_ROWS, B_KV_RANK).astype(BF16)
    qp = qp_ref[...].reshape(SA_ROWS, B_ROPE).astype(BF16)

    @pl.when(ch == 0)
    def _():
        m_ref[...] = jnp.full_like(m_ref, -jnp.inf)
        l_ref[...] = jnp.zeros_like(l_ref)
        acc_ref[...] = jnp.zeros_like(acc_ref)

    for j in range(SA_PG):
        ckv = ckv_refs[j][...].astype(BF16)
        kpe = kpe_refs[j][...].astype(BF16)
        s_ref[:, j * PAGE_SIZE:(j + 1) * PAGE_SIZE] = (
            lax.dot_general(ql, ckv, dn, preferred_element_type=F32)
            + lax.dot_general(qp, kpe, dn, preferred_element_type=F32)) * B_SCALE
    s = s_ref[...]
    m_old = m_ref[...]
    m_new = jnp.maximum(m_old, jnp.max(s, axis=-1, keepdims=True))
    alpha = jnp.exp(m_old - m_new)
    p = jnp.exp(s - m_new)
    l_ref[...] = alpha * l_ref[...] + jnp.sum(p, axis=-1, keepdims=True)
    pb = p.astype(BF16)
    acc = alpha * acc_ref[...]
    for j in range(SA_PG):
        acc = acc + jnp.dot(pb[:, j * PAGE_SIZE:(j + 1) * PAGE_SIZE], ckv_refs[j][...].astype(BF16),
                            preferred_element_type=F32)
    acc_ref[...] = acc
    m_ref[...] = m_new

    @pl.when(ch == pl.num_programs(1) - 1)
    def _():
        cn = cn_ref[...].astype(BF16)
        kn = kn_ref[...].astype(BF16)
        sn = (lax.dot_general(ql, cn, dn, preferred_element_type=F32)
              + lax.dot_general(qp, kn, dn, preferred_element_type=F32)) * B_SCALE
        t_q = lax.broadcasted_iota(jnp.int32, (B_HEADS, DEC_SEQ, DEC_SEQ), 1).reshape(SA_ROWS, DEC_SEQ)
        t_k = lax.broadcasted_iota(jnp.int32, (SA_ROWS, DEC_SEQ), 1)
        sn = jnp.where(t_k <= t_q, sn, -jnp.inf)
        m_o = m_ref[...]
        m_f = jnp.maximum(m_o, jnp.max(sn, axis=-1, keepdims=True))
        al = jnp.exp(m_o - m_f)
        pn = jnp.exp(sn - m_f)
        l_f = al * l_ref[...] + jnp.sum(pn, axis=-1, keepdims=True)
        acc_f = al * acc_ref[...] + jnp.dot(pn.astype(BF16), cn, preferred_element_type=F32)
        _uv_project(acc_f, l_f, wkv_ref, o_ref, DEC_SEQ)


def _sample_attn(page_table_flat, q_lat, q_pe, ckv, kpe, w_ukv_l, cache_ckv, cache_kpe, layer):
    n_pool = cache_ckv.shape[1]
    cc = cache_ckv.reshape(DEPTH * n_pool, PAGE_SIZE, B_KV_RANK)
    ck = cache_kpe.reshape(DEPTH * n_pool, PAGE_SIZE, B_ROPE)
    rb0 = M_P // DEC_SEQ

    def page_idx(j):
        return lambda b, c, pt: (layer * n_pool + pt[b * N_PAGES + c * SA_PG + j], 0, 0)

    in_specs = [
        pl.BlockSpec((B_HEADS, DEC_SEQ, B_KV_RANK), lambda b, c, pt: (0, rb0 + b, 0)),
        pl.BlockSpec((B_HEADS, DEC_SEQ, B_ROPE), lambda b, c, pt: (0, rb0 + b, 0)),
        pl.BlockSpec((DEC_SEQ, B_KV_RANK), lambda b, c, pt: (rb0 + b, 0)),
        pl.BlockSpec((DEC_SEQ, B_ROPE), lambda b, c, pt: (rb0 + b, 0)),
        pl.BlockSpec((B_KV_RANK, B_HEADS * (B_NOPE + B_V)), lambda b, c, pt: (0, 0)),
    ]
    in_specs += [pl.BlockSpec((None, PAGE_SIZE, B_KV_RANK), page_idx(j)) for j in range(SA_PG)]
    in_specs += [pl.BlockSpec((None, PAGE_SIZE, B_ROPE), page_idx(j)) for j in range(SA_PG)]
    grid_spec = pltpu.PrefetchScalarGridSpec(
        num_scalar_prefetch=1,
        grid=(DEC_BATCH, SA_NCH),
        in_specs=in_specs,
        out_specs=pl.BlockSpec((DEC_SEQ, B_HEADS * B_V), lambda b, c, pt: (b, 0)),
        scratch_shapes=[pltpu.VMEM((SA_ROWS, 1), F32), pltpu.VMEM((SA_ROWS, 1), F32),
                        pltpu.VMEM((SA_ROWS, B_KV_RANK), F32), pltpu.VMEM((SA_ROWS, SA_PG * PAGE_SIZE), F32)],
    )
    return pl.pallas_call(
        _sattn_kernel,
        grid_spec=grid_spec,
        out_shape=jax.ShapeDtypeStruct((M_S, B_HEADS * B_V), F32),
        compiler_params=_cparams("arbitrary", "arbitrary"),
        name="sample_attn",
    )(page_table_flat, q_lat, q_pe, ckv, kpe, w_ukv_l, *([cc] * SA_PG), *([ck] * SA_PG))


def _cmlp_kernel(u_ref, v_ref, nw_ref, wp_ref, ws_ref, bp_ref, bs_ref, o_ref, vn_ref):
    i = pl.program_id(0)
    u = _gelu(u_ref[...])
    v = _gelu(v_ref[...])
    vn = v * lax.rsqrt(jnp.mean(v * v, axis=-1, keepdims=True) + EPS) * nw_ref[...]
    vn_ref[...] = vn
    vb = vn.astype(BF16)

    def run(w_ref, b_ref):
        for g in range(C_GROUPS):
            cols = slice(g * C_GROUP_DIM, (g + 1) * C_GROUP_DIM)
            mixed = jnp.dot(w_ref[g], vb[:, cols], preferred_element_type=F32) + b_ref[g]
            o_ref[:, cols] = u[:, cols] * mixed

    @pl.when(i < NT_P)
    def _():
        run(wp_ref, bp_ref)

    @pl.when(i >= NT_P)
    def _():
        run(ws_ref, bs_ref)


def _block_diag_ws(w_s_l, b_s_l, length):
    reps = TM // length
    wm = jnp.tril(w_s_l[:, :length, :length])
    eye = jnp.eye(reps, dtype=F32)
    wbd = jnp.einsum("ab,gts->gatbs", eye, wm).reshape(C_GROUPS, TM, TM).astype(BF16)
    bcol = jnp.tile(b_s_l[:, :length], (1, reps)).reshape(C_GROUPS, TM, 1)
    return wbd, bcol


def _chunk_mlp(z, c_vnorm_l, w_s_l, b_s_l):
    wp, bp = _block_diag_ws(w_s_l, b_s_l, C_CHUNK)
    ws, bs = _block_diag_ws(w_s_l, b_s_l, DEC_SEQ)
    wspec = pl.BlockSpec((C_GROUPS, TM, TM), lambda i: (0, 0, 0))
    bspec = pl.BlockSpec((C_GROUPS, TM, 1), lambda i: (0, 0, 0))
    row = pl.BlockSpec((TM, C_WIDTH), lambda i: (i, 0))
    return pl.pallas_call(
        _cmlp_kernel,
        grid=(NT,),
        in_specs=[
            pl.BlockSpec((TM, C_WIDTH), lambda i: (i, Z_U // C_WIDTH)),
            pl.BlockSpec((TM, C_WIDTH), lambda i: (i, Z_V // C_WIDTH)),
            pl.BlockSpec((1, C_WIDTH), lambda i: (0, 0)),
            wspec, wspec, bspec, bspec,
        ],
        out_specs=[row, row],
        out_shape=[jax.ShapeDtypeStruct((M_ALL, C_WIDTH), F32), jax.ShapeDtypeStruct((M_ALL, C_WIDTH), F32)],
        compiler_params=_cparams("arbitrary"),
        name="chunk_mlp",
    )(z, z, c_vnorm_l.reshape(1, -1), wp, ws, bp, bs)


def _prep_w_in(w):
    o = 0
    segs = {}
    for name, width in (("xqkv", A_CONV_CH), ("gate", A_V_WIDTH), ("a", A_HEADS), ("bt", A_HEADS), ("cq", B_Q_RANK),
                        ("ckv", B_KV_RANK), ("kpe", B_ROPE), ("u", C_WIDTH), ("v", C_WIDTH)):
        segs[name] = w[:, o:o + width]
        o += width
    zeros = lambda n: jnp.zeros((w.shape[0], n), w.dtype)
    parts = [segs["xqkv"], segs["gate"], segs["u"], segs["v"], segs["ckv"], segs["cq"],
             segs["kpe"], zeros(LANES - B_ROPE), segs["a"], segs["bt"], zeros(LANES - 2 * A_HEADS)]
    out = jnp.concatenate(parts, axis=1).astype(BF16)
    assert out.shape[1] == Z_WIDTH
    return out


def _prep_w_uq(w):
    w3 = w.reshape(B_Q_RANK, B_HEADS, B_NOPE + B_ROPE)
    return jnp.concatenate([w3[:, :, :B_NOPE].reshape(B_Q_RANK, NOPE_W), w3[:, :, B_NOPE:].reshape(B_Q_RANK, ROPE_W)],
                           axis=1)


def _rope_tables():
    half = B_ROPE // 2
    inv = 1.0 / (ROPE_THETA ** (jnp.arange(half, dtype=F32) / half))
    pos = jnp.concatenate([jnp.tile(jnp.arange(SEQ, dtype=jnp.int32), BATCH),
                           jnp.tile(PAST_LEN + jnp.arange(DEC_SEQ, dtype=jnp.int32), DEC_BATCH)])
    ang = pos.astype(F32)[:, None] * inv[None, :]
    cos, sin = jnp.cos(ang), jnp.sin(ang)
    cos_t = jnp.tile(jnp.concatenate([cos, cos], axis=1), (1, B_HEADS))
    sin_t = jnp.tile(jnp.concatenate([-sin, sin], axis=1), (1, B_HEADS))
    return cos_t, sin_t


def _lane_row(vec):
    return jnp.zeros((1, LANES), F32).at[0, :vec.shape[0]].set(vec.astype(F32))


def kernel(x_prompt, x_sample, cache_ckv, cache_kpe, state_gdn, state_conv, page_table, c_prompt, c_sample, w_ada, b_ada, w_in, conv_w, a_log, dt_bias, gdn_norm, q_norm, kv_norm, w_uq, w_ukv, c_vnorm, w_s, b_s, w_out, w_gate, w_up, w_down, w_router, e_gate, e_up, e_down, final_norm):
    x = jnp.concatenate([x_prompt.reshape(M_P, D_MODEL), x_sample.reshape(M_S, D_MODEL)], axis=0)
    c_all = jnp.concatenate([c_prompt, c_sample, jnp.zeros((N_COND_PAD - N_COND, D_MODEL), F32)], axis=0)
    mod = _ada(c_all, w_ada, b_ada)
    cos_t, sin_t = _rope_tables()
    pt_flat = page_table.reshape(-1)
    dense_te = jnp.zeros((NT,), jnp.int32)
    dense_nv = jnp.full((1,), NT, jnp.int32)

    def mods(l):
        return (mod[l, :BATCH].reshape(BATCH, 6, 1, D_MODEL), mod[l, BATCH:N_COND].reshape(DEC_BATCH, 6, 1, D_MODEL))

    outs = {k: [] for k in ("p_ckv", "p_kpe", "p_gdn", "p_conv", "s_ckv", "s_kpe", "s_gdn", "s_conv", "s_cv")}
    mp, ms = mods(0)
    h = _normmod(x, mp, ms, 0, 1)
    y = None
    for l in range(DEPTH):
        z = _mm(h, _prep_w_in(w_in[l]), 0, tn=1024, name="in_proj")
        st_pad = jnp.concatenate([jnp.zeros((DEC_BATCH, SUBLANES - (A_CONV - 1), A_CONV_CH), F32), state_conv[l]],
                                 axis=1).reshape(M_S, A_CONV_CH)
        qkv = _conv_qkv(z, st_pad, conv_w[l])
        gate_rows = (_lane_row(a_log[l]), _lane_row(dt_bias[l]), gdn_norm[l].reshape(1, A_DV).astype(F32))
        oa_p, sg_p = _gdn(qkv, z, gate_rows, None, row0=0, nseq=BATCH, seq_len=SEQ, c=A_CHUNK, hb=4, hi=True,
                          name="gdn_prompt")
        oa_s, sg_s = _gdn(qkv, z, gate_rows, state_gdn.reshape((-1,) + state_gdn.shape[2:]), row0=M_P,
                          nseq=DEC_BATCH, seq_len=DEC_SEQ, c=DEC_SEQ, hb=A_HEADS, hi=False, name="gdn_sample",
                          s0_seq0=l * DEC_BATCH)
        q_lat, q_pe, ckv, kpe = _mla_proj(z, cos_t, sin_t, q_norm[l], kv_norm[l], _prep_w_uq(w_uq[l]), w_ukv[l])
        ob_p = _prompt_attn(q_lat, q_pe, ckv, kpe, w_ukv[l])
        ob_s = _sample_attn(pt_flat, q_lat, q_pe, ckv, kpe, w_ukv[l], cache_ckv, cache_kpe, l)
        o_c, vn = _chunk_mlp(z, c_vnorm[l], w_s[l], b_s[l])
        cat = jnp.concatenate([jnp.concatenate([oa_p, oa_s], axis=0), jnp.concatenate([ob_p, ob_s], axis=0), o_c],
                              axis=1).astype(BF16)
        f = _mm(cat, w_out, l, tn=512, name="out_proj")
        x, h = _resid(x, [f], None, mp, ms, 2, nxt=(mp, ms, 3, 4))

        xq = z[:, :A_CONV_CH]
        outs["p_ckv"].append(ckv[:M_P].reshape(BATCH, SEQ, B_KV_RANK))
        outs["p_kpe"].append(kpe[:M_P].reshape(BATCH, SEQ, B_ROPE))
        outs["p_gdn"].append(sg_p)
        outs["p_conv"].append(xq[:M_P].reshape(BATCH, SEQ, A_CONV_CH)[:, SEQ - (A_CONV - 1):])
        outs["s_ckv"].append(ckv[M_P:].reshape(DEC_BATCH, DEC_SEQ, B_KV_RANK))
        outs["s_kpe"].append(kpe[M_P:].reshape(DEC_BATCH, DEC_SEQ, B_ROPE))
        outs["s_gdn"].append(sg_s)
        outs["s_conv"].append(xq[M_P:].reshape(DEC_BATCH, DEC_SEQ, A_CONV_CH)[:, DEC_SEQ - (A_CONV - 1):])
        outs["s_cv"].append(vn[M_P:].reshape(DEC_BATCH, DEC_SEQ, C_WIDTH))

        m_idx = l // 2
        if l % 2 == 0:
            hid = _swiglu_up(h, w_gate.reshape((-1, 1) + w_gate.shape[1:]), w_up.reshape((-1, 1) + w_up.shape[1:]),
                             m_idx, dense_te, dense_nv, tm=TM)
            fs = [_swiglu_down(hid, w_down.reshape((-1, 1) + w_down.shape[1:]), m_idx, dense_te[:M_ALL // DENSE_DOWN_TM],
                               jnp.full((1,), M_ALL // DENSE_DOWN_TM, jnp.int32), tm=DENSE_DOWN_TM)]
            tw = None
        else:
            w_r = jnp.zeros((D_MODEL, LANES), BF16).at[:, :N_EXPERTS].set(w_router[m_idx].astype(BF16))
            ti, tw = _router(h, w_r)
            src, pos, tile_e, n_valid = _route_plan(ti[:, :TOP_K])
            xs = jnp.take(h, src, axis=0)
            hid = _swiglu_up(xs, e_gate, e_up, m_idx, tile_e, n_valid, tm=MOE_TM)
            ys = _swiglu_down(hid, e_down, m_idx, tile_e, n_valid, tm=MOE_TM)
            fs = [jnp.take(ys, pos[:, 0], axis=0), jnp.take(ys, pos[:, 1], axis=0)]
        if l + 1 < DEPTH:
            mp_n, ms_n = mods(l + 1)
            x, h = _resid(x, fs, tw, mp, ms, 5, nxt=(mp_n, ms_n, 0, 1))
            mp, ms = mp_n, ms_n
        else:
            y = _resid(x, fs, tw, mp, ms, 5, final_w=final_norm)

    st = lambda k: jnp.stack(outs[k])
    return (y[:M_P].reshape(BATCH, SEQ, D_MODEL), y[M_P:].reshape(DEC_BATCH, DEC_SEQ, D_MODEL),
            st("p_ckv"), st("p_kpe"), st("p_gdn"), st("p_conv"),
            st("s_ckv"), st("s_kpe"), st("s_gdn"), st("s_conv"), st("s_cv"))
```

```python
import functools

import numpy as np
import jax
import jax.numpy as jnp
from jax import lax
from jax.experimental import pallas as pl
from jax.experimental.pallas import tpu as pltpu

D_MODEL = 4096
BATCH = 4
SEQ = 2048
DEPTH = 2
DEC_BATCH = 128
DEC_SEQ = 8
PAST_LEN = 16384
PAGE_SIZE = 128
A_HEADS = 12
A_DK = 128
A_DV = 128
A_CONV = 4
A_CHUNK = 64
B_HEADS = 12
B_Q_RANK = 768
B_KV_RANK = 256
B_NOPE = 128
B_ROPE = 64
B_V = 128
ROPE_THETA = 10000.0
C_GROUPS = 8
C_GROUP_DIM = 128
C_CHUNK = 128
D_FF = 14336
N_EXPERTS = 8
TOP_K = 2
EPS = 1e-6

A_QK_WIDTH = A_HEADS * A_DK
A_V_WIDTH = A_HEADS * A_DV
A_CONV_CH = 2 * A_QK_WIDTH + A_V_WIDTH
C_WIDTH = C_GROUPS * C_GROUP_DIM
MIX_WIDTH = A_V_WIDTH + B_HEADS * B_V + C_WIDTH
B_SCALE = (B_NOPE + B_ROPE) ** -0.5
N_PAGES = PAST_LEN // PAGE_SIZE

F32 = jnp.float32
BF16 = jnp.bfloat16
LANES = 128
SUBLANES = 8
VMEM_LIMIT = 56 * 1024 * 1024

M_P = BATCH * SEQ
M_S = DEC_BATCH * DEC_SEQ
M_ALL = M_P + M_S
N_COND = BATCH + DEC_BATCH
N_COND_PAD = -(-N_COND // SUBLANES) * SUBLANES

TM = 512
NT_P = M_P // TM
NT = M_ALL // TM
TE = 256
NE_P = M_P // TE
NE = M_ALL // TE
SEQ_TE = SEQ // TE
SB_TE = TE // DEC_SEQ

Z_XQKV = 0
Z_GATE = 4608
Z_U = 6144
Z_V = 7168
Z_CKV = 8192
Z_CQ = 8448
Z_KPE = 9216
Z_AB = 9344
Z_WIDTH = 9472


def _cparams(*sem):
    return pltpu.CompilerParams(dimension_semantics=sem, vmem_limit_bytes=VMEM_LIMIT)


def _silu(x):
    return x * jax.nn.sigmoid(x)


def _gelu(x):
    return 0.5 * x * (1.0 + lax.erf(x * (2.0 ** -0.5)))


def _cast_rows(dst_ref, src_ref, chunk=256):
    rows = src_ref.shape[0]
    chunk = min(chunk, rows)

    def body(i, c):
        r = pl.multiple_of(i * chunk, chunk)
        dst_ref[pl.ds(r, chunk), :] = src_ref[pl.ds(r, chunk), :].astype(dst_ref.dtype)
        return c

    lax.fori_loop(0, rows // chunk, body, 0)


def _ada_kernel(c_ref, w_ref, b_ref, o_ref):
    a = _silu(c_ref[...]).astype(BF16)
    o_ref[...] = jnp.dot(a, w_ref[...].astype(BF16), preferred_element_type=F32) + b_ref[...]


def _ada(c_all, w_ada, b_ada):
    nl = w_ada.shape[0]
    tn = 512
    n6 = 6 * D_MODEL
    return pl.pallas_call(
        _ada_kernel,
        grid=(nl, n6 // tn),
        in_specs=[
            pl.BlockSpec((N_COND_PAD, D_MODEL), lambda l, n: (0, 0)),
            pl.BlockSpec((None, D_MODEL, tn), lambda l, n: (l, 0, n)),
            pl.BlockSpec((None, 1, tn), lambda l, n: (l, 0, n)),
        ],
        out_specs=pl.BlockSpec((None, N_COND_PAD, tn), lambda l, n: (l, 0, n)),
        out_shape=jax.ShapeDtypeStruct((nl, N_COND_PAD, n6), F32),
        compiler_params=_cparams("arbitrary", "arbitrary"),
        name="ada",
    )(c_all, w_ada, b_ada.reshape(nl, 1, n6))


def _mod_specs(which):
    sp = pl.BlockSpec((1, None, 1, D_MODEL), lambda i: (jnp.minimum(i // SEQ_TE, BATCH - 1), which, 0, 0))
    ss = pl.BlockSpec((SB_TE, None, 1, D_MODEL), lambda i: (jnp.maximum(i - NE_P, 0), which, 0, 0))
    return sp, ss


def _per_group(i, fn):
    @pl.when(i < NE_P)
    def _():
        fn(0)

    @pl.when(i >= NE_P)
    def _():
        fn(1)


def _rms(x3):
    return x3 * lax.rsqrt(jnp.mean(x3 * x3, axis=-1, keepdims=True) + EPS)


def _normmod_kernel(x_ref, shp_ref, shs_ref, scp_ref, scs_ref, h_ref):
    i = pl.program_id(0)

    def run(g):
        ns = (1, SB_TE)[g]
        sh = (shp_ref, shs_ref)[g][...]
        sc = (scp_ref, scs_ref)[g][...]
        x3 = x_ref[...].reshape(ns, TE // ns, D_MODEL)
        h = _rms(x3) * (1.0 + sc) + sh
        h_ref[...] = h.reshape(TE, D_MODEL).astype(h_ref.dtype)

    _per_group(i, run)


def _normmod(x, mod_p, mod_s, i_shift, i_scale):
    shp, shs = _mod_specs(i_shift)
    scp, scs = _mod_specs(i_scale)
    row = pl.BlockSpec((TE, D_MODEL), lambda i: (i, 0))
    return pl.pallas_call(
        _normmod_kernel,
        grid=(NE,),
        in_specs=[row, shp, shs, scp, scs],
        out_specs=row,
        out_shape=jax.ShapeDtypeStruct((M_ALL, D_MODEL), BF16),
        compiler_params=_cparams("arbitrary"),
        name="normmod",
    )(x, mod_p, mod_s, mod_p, mod_s)


def _resid_kernel(*refs, n_f, mode):
    i = pl.program_id(0)
    x_ref, f0_ref = refs[0], refs[1]
    k = 2
    if n_f == 2:
        f1_ref, tw_ref = refs[2], refs[3]
        k = 4
    gp_ref, gs_ref = refs[k], refs[k + 1]
    if mode == "mod":
        shp_ref, shs_ref, scp_ref, scs_ref, y_ref, h_ref = refs[k + 2:k + 8]
    else:
        fw_ref, out_ref = refs[k + 2:k + 4]

    def run(g):
        ns = (1, SB_TE)[g]
        f = f0_ref[...]
        if n_f == 2:
            tw = tw_ref[...]
            f = f * tw[:, 0:1] + f1_ref[...] * tw[:, 1:2]
        shape3 = (ns, TE // ns, D_MODEL)
        gate = (gp_ref, gs_ref)[g][...]
        y3 = x_ref[...].reshape(shape3) + gate * f.reshape(shape3)
        if mode == "mod":
            sh = (shp_ref, shs_ref)[g][...]
            sc = (scp_ref, scs_ref)[g][...]
            y_ref[...] = y3.reshape(TE, D_MODEL)
            h_ref[...] = (_rms(y3) * (1.0 + sc) + sh).reshape(TE, D_MODEL).astype(h_ref.dtype)
        else:
            out_ref[...] = (_rms(y3) * fw_ref[...]).reshape(TE, D_MODEL)

    _per_group(i, run)


def _resid(x, fs, tw, mod_p, mod_s, i_gate, nxt=None, final_w=None):
    row = pl.BlockSpec((TE, D_MODEL), lambda i: (i, 0))
    n_f = len(fs)
    args = [x] + list(fs)
    specs = [row] * (1 + n_f)
    if n_f == 2:
        args.append(tw)
        specs.append(pl.BlockSpec((TE, LANES), lambda i: (i, 0)))
    gp, gs = _mod_specs(i_gate)
    args += [mod_p, mod_s]
    specs += [gp, gs]
    if nxt is not None:
        np_, ns_, i_shift, i_scale = nxt
        shp, shs = _mod_specs(i_shift)
        scp, scs = _mod_specs(i_scale)
        args += [np_, ns_, np_, ns_]
        specs += [shp, shs, scp, scs]
        out_specs = [row, row]
        out_shape = [jax.ShapeDtypeStruct((M_ALL, D_MODEL), F32), jax.ShapeDtypeStruct((M_ALL, D_MODEL), BF16)]
        mode = "mod"
    else:
        args.append(final_w.reshape(1, 1, D_MODEL))
        specs.append(pl.BlockSpec((1, 1, D_MODEL), lambda i: (0, 0, 0)))
        out_specs = row
        out_shape = jax.ShapeDtypeStruct((M_ALL, D_MODEL), F32)
        mode = "final"
    return pl.pallas_call(
        functools.partial(_resid_kernel, n_f=n_f, mode=mode),
        grid=(NE,),
        in_specs=specs,
        out_specs=out_specs,
        out_shape=out_shape,
        compiler_params=_cparams("arbitrary"),
        name="resid_" + mode,
    )(*args)


def _mm_kernel(x_ref, w_ref, o_ref, *scratch, cast_w):
    if cast_w:
        wb_ref, = scratch

        @pl.when(pl.program_id(1) == 0)
        def _():
            _cast_rows(wb_ref, w_ref)

        w = wb_ref[...]
    else:
        w = w_ref[...]
    o_ref[...] = jnp.dot(x_ref[...], w, preferred_element_type=F32).astype(o_ref.dtype)


def _mm(x, w, layer, tn, out_dtype=F32, name="mm"):
    m, k = x.shape
    n = w.shape[-1]
    cast_w = w.dtype != BF16
    if w.ndim == 3:
        w_spec = pl.BlockSpec((None, k, tn), lambda j, i: (layer, 0, j))
    else:
        w_spec = pl.BlockSpec((k, tn), lambda j, i: (0, j))
    scratch = [pltpu.VMEM((k, tn), BF16)] if cast_w else []
    return pl.pallas_call(
        functools.partial(_mm_kernel, cast_w=cast_w),
        grid=(pl.cdiv(n, tn), m // TM),
        in_specs=[pl.BlockSpec((TM, k), lambda j, i: (i, 0)), w_spec],
        out_specs=pl.BlockSpec((TM, tn), lambda j, i: (i, j)),
        out_shape=jax.ShapeDtypeStruct((m, n), out_dtype),
        scratch_shapes=scratch,
        compiler_params=_cparams("arbitrary", "arbitrary"),
        name=name,
    )(x, w)


UP_TM = 512
DOWN_TM = 1024
DOWN_SUB = 512
assert M_ALL % DOWN_TM == 0


def _swiglu_kernel(plan_ref, x_ref, wg_ref, wu_ref, o_ref, wgb_ref, wub_ref):
    i = pl.program_id(1)
    e_prev = plan_ref[0, jnp.maximum(i - 1, 0)]
    changed = jnp.logical_or(i == 0, plan_ref[0, i] != e_prev)
    used = plan_ref[1, i] > 0

    @pl.when(changed)
    def _():
        _cast_rows(wgb_ref, wg_ref)
        _cast_rows(wub_ref, wu_ref)

    @pl.when(used)
    def _():
        x = x_ref[...]
        g = jnp.dot(x, wgb_ref[...], preferred_element_type=F32)
        u = jnp.dot(x, wub_ref[...], preferred_element_type=F32)
        o_ref[...] = (_silu(g) * u).astype(o_ref.dtype)

    @pl.when(jnp.logical_not(used))
    def _():
        o_ref[...] = jnp.zeros_like(o_ref)


def _swiglu_up(x, w_gate, w_up, layer, plan, tn=512):
    m = x.shape[0]
    w_spec = pl.BlockSpec((None, None, D_MODEL, tn), lambda j, i, pr: (layer, pr[0, i], 0, j))
    grid_spec = pltpu.PrefetchScalarGridSpec(
        num_scalar_prefetch=1,
        grid=(D_FF // tn, m // UP_TM),
        in_specs=[pl.BlockSpec((UP_TM, D_MODEL), lambda j, i, pr: (pr[2, i], 0)), w_spec, w_spec],
        out_specs=pl.BlockSpec((UP_TM, tn), lambda j, i, pr: (i, j)),
        scratch_shapes=[pltpu.VMEM((D_MODEL, tn), BF16), pltpu.VMEM((D_MODEL, tn), BF16)],
    )
    return pl.pallas_call(
        _swiglu_kernel,
        grid_spec=grid_spec,
        out_shape=jax.ShapeDtypeStruct((m, D_FF), BF16),
        compiler_params=_cparams("arbitrary", "arbitrary"),
        name="swiglu_up",
    )(plan, x, w_gate, w_up)


def _down_kernel(plan_ref, x_ref, w_ref, o_ref):
    i = pl.program_id(0)
    kk = pl.program_id(2)
    nsub = plan_ref[1, i]

    def accumulate(rows):
        p = jnp.dot(x_ref[0:rows, :], w_ref[...].astype(BF16), preferred_element_type=F32)

        @pl.when(kk == 0)
        def _():
            o_ref[0:rows, :] = p

        @pl.when(kk > 0)
        def _():
            o_ref[0:rows, :] += p

    @pl.when(nsub == 2)
    def _():
        accumulate(DOWN_TM)

    @pl.when(nsub == 1)
    def _():
        accumulate(DOWN_SUB)

    @pl.when(jnp.logical_and(kk == 0, nsub < 2))
    def _():
        o_ref[DOWN_SUB:, :] = jnp.zeros((DOWN_TM - DOWN_SUB, o_ref.shape[1]), o_ref.dtype)

    @pl.when(jnp.logical_and(kk == 0, nsub < 1))
    def _():
        o_ref[0:DOWN_SUB, :] = jnp.zeros((DOWN_SUB, o_ref.shape[1]), o_ref.dtype)


def _swiglu_down(hid, w_down, layer, plan, tn=1024, tk=1024):
    m = hid.shape[0]
    grid_spec = pltpu.PrefetchScalarGridSpec(
        num_scalar_prefetch=1,
        grid=(m // DOWN_TM, D_MODEL // tn, D_FF // tk),
        in_specs=[
            pl.BlockSpec((DOWN_TM, tk), lambda i, j, k, pr: (pr[2, i], k)),
            pl.BlockSpec((None, None, tk, tn), lambda i, j, k, pr: (layer, pr[0, i], k, j)),
        ],
        out_specs=pl.BlockSpec((DOWN_TM, tn), lambda i, j, k, pr: (i, j)),
    )
    return pl.pallas_call(
        _down_kernel,
        grid_spec=grid_spec,
        out_shape=jax.ShapeDtypeStruct((m, D_MODEL), F32),
        compiler_params=_cparams("arbitrary", "arbitrary", "arbitrary"),
        name="swiglu_down",
    )(plan, hid, w_down)


def _dense_plan(tiles):
    t = jnp.arange(tiles, dtype=jnp.int32)
    return jnp.stack([jnp.zeros_like(t), jnp.full_like(t, 2), t])


def _router_kernel(h_ref, w_ref, ti_ref, tw_ref):
    logits = jnp.dot(h_ref[...], w_ref[...], preferred_element_type=F32)
    lane = lax.broadcasted_iota(jnp.int32, logits.shape, 1)
    neg = jnp.float32(-jnp.inf)
    logits = jnp.where(lane < N_EXPERTS, logits, neg)
    m1 = jnp.max(logits, axis=-1, keepdims=True)
    i1 = jnp.min(jnp.where(logits == m1, lane, LANES), axis=-1, keepdims=True)
    rest = jnp.where(lane == i1, neg, logits)
    m2 = jnp.max(rest, axis=-1, keepdims=True)
    i2 = jnp.min(jnp.where(rest == m2, lane, LANES), axis=-1, keepdims=True)
    e2 = jnp.exp(m2 - m1)
    den = 1.0 + e2
    ti_ref[...] = jnp.where(lane == 0, i1, jnp.where(lane == 1, i2, 0))
    tw_ref[...] = jnp.where(lane == 0, 1.0 / den, jnp.where(lane == 1, e2 / den, 0.0))


def _router(h, w_router_pad):
    row = pl.BlockSpec((TM, LANES), lambda i: (i, 0))
    return pl.pallas_call(
        _router_kernel,
        grid=(NT,),
        in_specs=[pl.BlockSpec((TM, D_MODEL), lambda i: (i, 0)), pl.BlockSpec((D_MODEL, LANES), lambda i: (0, 0))],
        out_specs=[row, row],
        out_shape=[jax.ShapeDtypeStruct((M_ALL, LANES), jnp.int32), jax.ShapeDtypeStruct((M_ALL, LANES), F32)],
        compiler_params=_cparams("arbitrary"),
        name="router",
    )(h, w_router_pad)


MOE_ROWS = M_ALL * TOP_K + N_EXPERTS * DOWN_TM


def _tile_plan(counts, starts, ends, tm, sub):
    tiles = MOE_ROWS // tm
    idx = jnp.arange(tiles, dtype=jnp.int32)
    t0 = idx * tm
    e = jnp.minimum(jnp.sum((t0[:, None] >= ends[None, :]).astype(jnp.int32), axis=1), N_EXPERTS - 1)
    used_rows = jnp.clip(counts[e] - (t0 - starts[e]), 0, tm)
    nsub = (used_rows + sub - 1) // sub
    fetch = lax.cummax(jnp.where(nsub > 0, idx, 0), axis=0)
    return jnp.stack([e, nsub, fetch]).astype(jnp.int32)


def _route_plan(ti):
    e_flat = ti.reshape(-1)
    onehot = (e_flat[:, None] == jnp.arange(N_EXPERTS, dtype=jnp.int32)[None, :]).astype(jnp.int32)
    counts = jnp.sum(onehot, axis=0)
    rank = jnp.sum((jnp.cumsum(onehot, axis=0) - onehot) * onehot, axis=1)
    padded = ((counts + DOWN_TM - 1) // DOWN_TM) * DOWN_TM
    ends = jnp.cumsum(padded)
    starts = ends - padded
    pos = starts[e_flat] + rank
    src = jnp.zeros((MOE_ROWS,), jnp.int32).at[pos].set(jnp.arange(e_flat.shape[0], dtype=jnp.int32) // TOP_K)
    return (src, pos.reshape(-1, TOP_K), _tile_plan(counts, starts, ends, UP_TM, UP_TM),
            _tile_plan(counts, starts, ends, DOWN_TM, DOWN_SUB))


def _conv_kernel(x_ref, hp_ref, hs_ref, w_ref, o_ref, sp_ref, ss_ref):
    i = pl.program_id(0)
    part = pl.program_id(1)
    width = x_ref.shape[1]
    w = w_ref[...]

    def finish(y):
        y = _silu(y)
        qk = part < 2
        post = jnp.where(part == 0, A_DK ** -0.5, 1.0)
        for h in range(width // A_DK):
            seg = y[:, h * A_DK:(h + 1) * A_DK]
            ss = jnp.sum(seg * seg, axis=-1, keepdims=True)
            scale = jnp.where(qk, lax.rsqrt(ss + EPS), 1.0) * post
            o_ref[:, h * A_DK:(h + 1) * A_DK] = seg * scale

    @pl.when(i < NE_P)
    def _():
        first = (i % SEQ_TE) == 0
        sp_ref[0:SUBLANES, :] = jnp.where(first, 0.0, hp_ref[...])
        sp_ref[SUBLANES:, :] = x_ref[...]
        acc = sp_ref[SUBLANES - 3:SUBLANES - 3 + TE, :] * w[0:1, :]
        for j in range(1, A_CONV):
            acc = acc + sp_ref[SUBLANES - 3 + j:SUBLANES - 3 + j + TE, :] * w[j:j + 1, :]
        finish(acc)

    @pl.when(i >= NE_P)
    def _():
        ss_ref[:, 0:SUBLANES, :] = hs_ref[...].reshape(SB_TE, SUBLANES, width)
        ss_ref[:, SUBLANES:, :] = x_ref[...].reshape(SB_TE, DEC_SEQ, width)
        acc = ss_ref[:, SUBLANES - 3:SUBLANES - 3 + DEC_SEQ, :] * w[0:1, :]
        for j in range(1, A_CONV):
            acc = acc + ss_ref[:, SUBLANES - 3 + j:SUBLANES - 3 + j + DEC_SEQ, :] * w[j:j + 1, :]
        finish(acc.reshape(TE, width))


def _conv_qkv(z, conv_state_pad, conv_w_l):
    width = A_QK_WIDTH
    assert DEC_SEQ == SUBLANES
    return pl.pallas_call(
        _conv_kernel,
        grid=(NE, 3),
        in_specs=[
            pl.BlockSpec((TE, width), lambda i, p: (i, p)),
            pl.BlockSpec((SUBLANES, width), lambda i, p: (jnp.maximum(i * (TE // SUBLANES) - 1, 0), p)),
            pl.BlockSpec((TE, width), lambda i, p: (jnp.maximum(i - NE_P, 0), p)),
            pl.BlockSpec((A_CONV, width), lambda i, p: (0, p)),
        ],
        out_specs=pl.BlockSpec((TE, width), lambda i, p: (i, p)),
        out_shape=jax.ShapeDtypeStruct((M_ALL, A_CONV_CH), F32),
        scratch_shapes=[pltpu.VMEM((TE + SUBLANES, width), F32), pltpu.VMEM((SB_TE, 2 * SUBLANES, width), F32)],
        compiler_params=_cparams("arbitrary", "arbitrary"),
        name="conv_qkv",
    )(z, z, conv_state_pad, conv_w_l)


GDN_HB_PROMPT = A_HEADS

_DN_NN = (((1,), (0,)), ((), ()))
_DN_NT = (((1,), (1,)), ((), ()))
_DN_TN = (((0,), (0,)), ((), ()))


def _dotp(a, b, prec, dn=_DN_NN):
    if prec == "f32":
        return lax.dot_general(a, b, dn, preferred_element_type=F32, precision=lax.Precision.HIGHEST)
    a_hi, b_hi = a.astype(BF16), b.astype(BF16)
    out = lax.dot_general(a_hi, b_hi, dn, preferred_element_type=F32)
    if prec == "bf16x3":
        a_lo = (a - a_hi.astype(F32)).astype(BF16)
        b_lo = (b - b_hi.astype(F32)).astype(BF16)
        out = out + (lax.dot_general(a_hi, b_lo, dn, preferred_element_type=F32)
                     + lax.dot_general(a_lo, b_hi, dn, preferred_element_type=F32))
    return out


def _gdn_kernel(*refs, c, hb, has_s0, p_inv, p_bulk):
    if has_s0:
        q_ref, k_ref, v_ref, zg_ref, ab_ref, al_ref, dt_ref, nw_ref, s0_ref, o_ref, sf_ref, s_ref = refs
    else:
        q_ref, k_ref, v_ref, zg_ref, ab_ref, al_ref, dt_ref, nw_ref, o_ref, sf_ref, s_ref = refs
    hblk = pl.program_id(1)
    ch = pl.program_id(2)

    @pl.when(ch == 0)
    def _():
        if has_s0:
            s_ref[...] = s0_ref[...]
        else:
            s_ref[...] = jnp.zeros_like(s_ref)

    ab = ab_ref[...]
    x = ab + dt_ref[...]
    softplus = jnp.maximum(x, 0.0) + jnp.log1p(jnp.exp(-jnp.abs(x)))
    g_all = -jnp.exp(al_ref[...]) * softplus
    beta_all = jax.nn.sigmoid(ab)
    r_i = lax.broadcasted_iota(jnp.int32, (c, c), 0)
    c_i = lax.broadcasted_iota(jnp.int32, (c, c), 1)
    incl = r_i >= c_i
    strict = r_i > c_i
    eye = (r_i == c_i).astype(F32)
    gc_all = _dotp(incl.astype(F32), g_all, "f32")
    sel = (lax.broadcasted_iota(jnp.int32, (2 * SUBLANES, LANES), 0)
           == lax.broadcasted_iota(jnp.int32, (2 * SUBLANES, LANES), 1)).astype(F32)
    gct_all = _dotp(sel, gc_all, "f32", _DN_NT)
    lane = lax.broadcasted_iota(jnp.int32, (c, LANES), 1)
    sub = lax.broadcasted_iota(jnp.int32, (2 * SUBLANES, c), 0)
    nw = nw_ref[...]

    hs = range(hb)
    cols = [slice(j * A_DK, (j + 1) * A_DK) for j in hs]
    heads = [hblk * hb + j for j in hs]
    q = [q_ref[:, cols[j]] for j in hs]
    k = [k_ref[:, cols[j]] for j in hs]
    v = [v_ref[:, cols[j]] for j in hs]
    s_old = [s_ref[j] for j in hs]
    gc = [jnp.sum(jnp.where(lane == heads[j], gc_all, 0.0), axis=1, keepdims=True) for j in hs]
    beta = [jnp.sum(jnp.where(lane == heads[j] + A_HEADS, beta_all, 0.0), axis=1, keepdims=True) for j in hs]
    gc_row = [jnp.sum(jnp.where(sub == heads[j], gct_all, 0.0), axis=0, keepdims=True) for j in hs]
    gc_last = [gc[j][c - 1:c, :] for j in hs]
    decay = [jnp.where(incl, jnp.exp(jnp.where(incl, gc[j] - gc_row[j], 0.0)), 0.0) for j in hs]
    kb = [k[j] * beta[j] for j in hs]
    x_pow = [-jnp.where(strict, _dotp(kb[j], k[j], p_bulk, _DN_NT) * decay[j], 0.0) for j in hs]
    qk = [_dotp(q[j], k[j], p_bulk, _DN_NT) * decay[j] for j in hs]
    t_inv = [eye + x_pow[j] for j in hs]
    for _ in range(max(int(np.ceil(np.log2(c))) - 1, 0)):
        x_pow = [_dotp(x_pow[j], x_pow[j], p_inv) for j in hs]
        t_inv = [t_inv[j] + _dotp(t_inv[j], x_pow[j], p_inv) for j in hs]
    eg = [jnp.exp(gc[j]) for j in hs]
    wu = [_dotp(t_inv[j], jnp.concatenate([kb[j] * eg[j], v[j] * beta[j]], axis=1), p_bulk) for j in hs]
    qs = [_dotp(q[j] * eg[j], s_old[j], p_bulk) for j in hs]
    v_new = [wu[j][:, A_DK:] - _dotp(wu[j][:, :A_DK], s_old[j], p_bulk) for j in hs]
    o = [qs[j] + _dotp(qk[j], v_new[j], p_bulk) for j in hs]
    kv = [_dotp(k[j] * jnp.exp(gc_last[j] - gc[j]), v_new[j], p_bulk, _DN_TN) for j in hs]
    for j in hs:
        s_ref[j] = s_old[j] * jnp.exp(gc_last[j]) + kv[j]
        o_n = o[j] * lax.rsqrt(jnp.mean(o[j] * o[j], axis=-1, keepdims=True) + EPS) * nw
        o_ref[:, cols[j]] = o_n * _silu(zg_ref[:, cols[j]])

    @pl.when(ch == pl.num_programs(2) - 1)
    def _():
        sf_ref[...] = s_ref[...]


def _gdn(qkv, z, gate_rows, s0, *, row0, nseq, seq_len, c, hb, name, s0_seq0=0, p_inv="bf16x3", p_bulk="bf16"):
    al_row, dt_row, nw_row = gate_rows
    nch = seq_len // c
    wblk = hb * A_DK
    rb0 = row0 // c
    nq = A_QK_WIDTH // wblk

    def rows(b, h, n):
        return rb0 + b * nch + n

    in_specs = [
        pl.BlockSpec((c, wblk), lambda b, h, n: (rows(b, h, n), h)),
        pl.BlockSpec((c, wblk), lambda b, h, n: (rows(b, h, n), nq + h)),
        pl.BlockSpec((c, wblk), lambda b, h, n: (rows(b, h, n), 2 * nq + h)),
        pl.BlockSpec((c, wblk), lambda b, h, n: (rows(b, h, n), Z_GATE // wblk + h)),
        pl.BlockSpec((c, LANES), lambda b, h, n: (rows(b, h, n), Z_AB // LANES)),
        pl.BlockSpec((1, LANES), lambda b, h, n: (0, 0)),
        pl.BlockSpec((1, LANES), lambda b, h, n: (0, 0)),
        pl.BlockSpec((1, LANES), lambda b, h, n: (0, 0)),
    ]
    args = [qkv, qkv, qkv, z, z, al_row, dt_row, nw_row]
    st_spec = pl.BlockSpec((None, hb, A_DK, A_DV), lambda b, h, n: (b, h, 0, 0))
    if s0 is not None:
        in_specs.append(pl.BlockSpec((None, hb, A_DK, A_DV), lambda b, h, n: (s0_seq0 + b, h, 0, 0)))
        args.append(s0)
    return pl.pallas_call(
        functools.partial(_gdn_kernel, c=c, hb=hb, has_s0=s0 is not None, p_inv=p_inv, p_bulk=p_bulk),
        grid=(nseq, A_HEADS // hb, nch),
        in_specs=in_specs,
        out_specs=[pl.BlockSpec((c, wblk), lambda b, h, n: (b * nch + n, h)), st_spec],
        out_shape=[jax.ShapeDtypeStruct((nseq * seq_len, A_V_WIDTH), F32),
                   jax.ShapeDtypeStruct((nseq, A_HEADS, A_DK, A_DV), F32)],
        scratch_shapes=[pltpu.VMEM((hb, A_DK, A_DV), F32)],
        compiler_params=_cparams("arbitrary", "arbitrary", "arbitrary"),
        name=name,
    )(*args)


ROPE_W = B_HEADS * B_ROPE
NOPE_W = B_HEADS * B_NOPE


def _rope_lanes(x, cos, sin_signed):
    w = x.shape[-1]
    half = B_ROPE // 2
    lane = lax.broadcasted_iota(jnp.int32, x.shape, 1)
    partner = jnp.where((lane % B_ROPE) < half, pltpu.roll(x, w - half, 1), pltpu.roll(x, half, 1))
    return x * cos + partner * sin_signed


def _mla_proj_kernel(cq_ref, ckv_ref, kpe_ref, cos_ref, sin_ref, qn_ref, kvn_ref, wq_ref, wkv_ref,
                     ql_ref, qp_ref, ckvo_ref, kpeo_ref, wqb_ref, wkvb_ref):
    @pl.when(pl.program_id(0) == 0)
    def _():
        _cast_rows(wqb_ref, wq_ref)
        _cast_rows(wkvb_ref, wkv_ref)

    cq = cq_ref[...]
    cqn = cq * lax.rsqrt(jnp.mean(cq * cq, axis=-1, keepdims=True) + EPS) * qn_ref[...]
    q = jnp.dot(cqn.astype(BF16), wqb_ref[...], preferred_element_type=F32)
    cos = cos_ref[...]
    sin = sin_ref[...]
    q_pe = _rope_lanes(q[:, NOPE_W:], cos, sin)
    for h in range(B_HEADS):
        q_nope = q[:, h * B_NOPE:(h + 1) * B_NOPE].astype(BF16)
        w_uk = wkvb_ref[:, h * (B_NOPE + B_V):h * (B_NOPE + B_V) + B_NOPE]
        ql_ref[h] = lax.dot_general(q_nope, w_uk, (((1,), (1,)), ((), ())),
                                    preferred_element_type=F32).astype(ql_ref.dtype)
        qp_ref[h] = q_pe[:, h * B_ROPE:(h + 1) * B_ROPE].astype(qp_ref.dtype)
    ckv = ckv_ref[...]
    ckvo_ref[...] = ckv * lax.rsqrt(jnp.mean(ckv * ckv, axis=-1, keepdims=True) + EPS) * kvn_ref[...]
    kpe = _rope_lanes(kpe_ref[...], cos[:, :LANES], sin[:, :LANES])
    kpeo_ref[...] = kpe[:, :B_ROPE]


def _mla_proj(z, cos_t, sin_t, q_norm_l, kv_norm_l, w_uq_perm, w_ukv_l):
    full = lambda shape: pl.BlockSpec(shape, lambda i: (0,) * len(shape))
    return pl.pallas_call(
        _mla_proj_kernel,
        grid=(NT,),
        in_specs=[
            pl.BlockSpec((TM, B_Q_RANK), lambda i: (i, Z_CQ // B_Q_RANK)),
            pl.BlockSpec((TM, B_KV_RANK), lambda i: (i, Z_CKV // B_KV_RANK)),
            pl.BlockSpec((TM, LANES), lambda i: (i, Z_KPE // LANES)),
            pl.BlockSpec((TM, ROPE_W), lambda i: (i, 0)),
            pl.BlockSpec((TM, ROPE_W), lambda i: (i, 0)),
            full((1, B_Q_RANK)), full((1, B_KV_RANK)),
            full((B_Q_RANK, NOPE_W + ROPE_W)), full((B_KV_RANK, B_HEADS * (B_NOPE + B_V))),
        ],
        out_specs=[
            pl.BlockSpec((B_HEADS, TM, B_KV_RANK), lambda i: (0, i, 0)),
            pl.BlockSpec((B_HEADS, TM, B_ROPE), lambda i: (0, i, 0)),
            pl.BlockSpec((TM, B_KV_RANK), lambda i: (i, 0)),
            pl.BlockSpec((TM, B_ROPE), lambda i: (i, 0)),
        ],
        out_shape=[
            jax.ShapeDtypeStruct((B_HEADS, M_ALL, B_KV_RANK), F32),
            jax.ShapeDtypeStruct((B_HEADS, M_ALL, B_ROPE), F32),
            jax.ShapeDtypeStruct((M_ALL, B_KV_RANK), F32),
            jax.ShapeDtypeStruct((M_ALL, B_ROPE), F32),
        ],
        scratch_shapes=[pltpu.VMEM((B_Q_RANK, NOPE_W + ROPE_W), BF16),
                        pltpu.VMEM((B_KV_RANK, B_HEADS * (B_NOPE + B_V)), BF16)],
        compiler_params=_cparams("arbitrary"),
        name="mla_proj",
    )(z, z, z, cos_t, sin_t, q_norm_l.reshape(1, -1), kv_norm_l.reshape(1, -1), w_uq_perm, w_ukv_l)


def _uv_project(acc, l, wkv_ref, o_ref, rows):
    o_lat = acc / l
    for h in range(B_HEADS):
        lo = h * (B_NOPE + B_V) + B_NOPE
        w_uv = wkv_ref[:, lo:lo + B_V].astype(BF16)
        o_h = o_lat[h * rows:(h + 1) * rows, :].astype(BF16)
        o_ref[:, h * B_V:(h + 1) * B_V] = jnp.dot(o_h, w_uv, preferred_element_type=F32)


PA_TQ = 128
PA_TK = 512


def _pattn_kernel(ql_ref, qp_ref, ckv_ref, kpe_ref, wkv_ref, o_ref, m_ref, l_ref, acc_ref):
    qb = pl.program_id(1)
    kb = pl.program_id(2)
    k_last = (qb * PA_TQ + PA_TQ - 1) // PA_TK
    rows = B_HEADS * PA_TQ

    @pl.when(kb == 0)
    def _():
        m_ref[...] = jnp.full_like(m_ref, -jnp.inf)
        l_ref[...] = jnp.zeros_like(l_ref)
        acc_ref[...] = jnp.zeros_like(acc_ref)

    @pl.when(kb <= k_last)
    def _():
        ql = ql_ref[...].reshape(rows, B_KV_RANK).astype(BF16)
        qp = qp_ref[...].reshape(rows, B_ROPE).astype(BF16)
        ckv = ckv_ref[...].astype(BF16)
        kpe = kpe_ref[...].astype(BF16)
        dn = (((1,), (1,)), ((), ()))
        s = (lax.dot_general(ql, ckv, dn, preferred_element_type=F32)
             + lax.dot_general(qp, kpe, dn, preferred_element_type=F32)) * B_SCALE
        qpos = qb * PA_TQ + lax.broadcasted_iota(jnp.int32, (B_HEADS, PA_TQ, PA_TK), 1).reshape(rows, PA_TK)
        kpos = kb * PA_TK + lax.broadcasted_iota(jnp.int32, (rows, PA_TK), 1)
        s = jnp.where(kpos <= qpos, s, -jnp.inf)
        m_old = m_ref[...]
        m_new = jnp.maximum(m_old, jnp.max(s, axis=-1, keepdims=True))
        alpha = jnp.exp(m_old - m_new)
        p = jnp.exp(s - m_new)
        l_ref[...] = alpha * l_ref[...] + jnp.sum(p, axis=-1, keepdims=True)
        acc_ref[...] = alpha * acc_ref[...] + jnp.dot(p.astype(BF16), ckv, preferred_element_type=F32)
        m_ref[...] = m_new

    @pl.when(kb == k_last)
    def _():
        _uv_project(acc_ref[...], l_ref[...], wkv_ref, o_ref, PA_TQ)


def _prompt_attn(q_lat, q_pe, ckv, kpe, w_ukv_l):
    nqb = SEQ // PA_TQ
    nkb = SEQ // PA_TK
    rows = B_HEADS * PA_TQ

    def kv_idx(b, qb, kb):
        return (b * nkb + jnp.minimum(kb, (qb * PA_TQ + PA_TQ - 1) // PA_TK), 0)

    return pl.pallas_call(
        _pattn_kernel,
        grid=(BATCH, nqb, nkb),
        in_specs=[
            pl.BlockSpec((B_HEADS, PA_TQ, B_KV_RANK), lambda b, qb, kb: (0, b * nqb + qb, 0)),
            pl.BlockSpec((B_HEADS, PA_TQ, B_ROPE), lambda b, qb, kb: (0, b * nqb + qb, 0)),
            pl.BlockSpec((PA_TK, B_KV_RANK), kv_idx),
            pl.BlockSpec((PA_TK, B_ROPE), kv_idx),
            pl.BlockSpec((B_KV_RANK, B_HEADS * (B_NOPE + B_V)), lambda b, qb, kb: (0, 0)),
        ],
        out_specs=pl.BlockSpec((PA_TQ, B_HEADS * B_V), lambda b, qb, kb: (b * nqb + qb, 0)),
        out_shape=jax.ShapeDtypeStruct((M_P, B_HEADS * B_V), F32),
        scratch_shapes=[pltpu.VMEM((rows, 1), F32), pltpu.VMEM((rows, 1), F32), pltpu.VMEM((rows, B_KV_RANK), F32)],
        compiler_params=_cparams("arbitrary", "arbitrary", "arbitrary"),
        name="prompt_attn",
    )(q_lat, q_pe, ckv, kpe, w_ukv_l)


SA_PG = 32
SA_NCH = N_PAGES // SA_PG
SA_ROWS = B_HEADS * DEC_SEQ


def _sattn_kernel(pt_ref, *refs):
    ql_ref, qp_ref, cn_ref, kn_ref, wkv_ref = refs[:5]
    ckv_refs = refs[5:5 + SA_PG]
    kpe_refs = refs[5 + SA_PG:5 + 2 * SA_PG]
    o_ref, m_ref, l_ref, acc_ref, kc_ref, kp_ref = refs[5 + 2 * SA_PG:]
    ch = pl.program_id(1)
    dn = (((1,), (1,)), ((), ()))
    ql = ql_ref[...].reshape(SA_ROWS, B_KV_RANK).astype(BF16)
    qp = qp_ref[...].reshape(SA_ROWS, B_ROPE).astype(BF16)

    @pl.when(ch == 0)
    def _():
        m_ref[...] = jnp.full_like(m_ref, -jnp.inf)
        l_ref[...] = jnp.zeros_like(l_ref)
        acc_ref[...] = jnp.zeros_like(acc_ref)

    for j in range(SA_PG):
        kc_ref[j * PAGE_SIZE:(j + 1) * PAGE_SIZE, :] = ckv_refs[j][...].astype(BF16)
        kp_ref[:, j * PAGE_SIZE:(j + 1) * PAGE_SIZE] = kpe_refs[j][...].astype(BF16)
    kc = kc_ref[...]
    s = (lax.dot_general(ql, kc, dn, preferred_element_type=F32)
         + jnp.dot(qp, kp_ref[...], preferred_element_type=F32)) * B_SCALE
    m_old = m_ref[...]
    m_new = jnp.maximum(m_old, jnp.max(s, axis=-1, keepdims=True))
    alpha = jnp.exp(m_old - m_new)
    p = jnp.exp(s - m_new)
    l_ref[...] = alpha * l_ref[...] + jnp.sum(p, axis=-1, keepdims=True)
    acc_ref[...] = alpha * acc_ref[...] + jnp.dot(p.astype(BF16), kc, preferred_element_type=F32)
    m_ref[...] = m_new

    @pl.when(ch == pl.num_programs(1) - 1)
    def _():
        cn = cn_ref[...].astype(BF16)
        kn = kn_ref[...].astype(BF16)
        sn = (lax.dot_general(ql, cn, dn, preferred_element_type=F32)
              + lax.dot_general(qp, kn, dn, preferred_element_type=F32)) * B_SCALE
        t_q = lax.broadcasted_iota(jnp.int32, (B_HEADS, DEC_SEQ, DEC_SEQ), 1).reshape(SA_ROWS, DEC_SEQ)
        t_k = lax.broadcasted_iota(jnp.int32, (SA_ROWS, DEC_SEQ), 1)
        sn = jnp.where(t_k <= t_q, sn, -jnp.inf)
        m_o = m_ref[...]
        m_f = jnp.maximum(m_o, jnp.max(sn, axis=-1, keepdims=True))
        al = jnp.exp(m_o - m_f)
        pn = jnp.exp(sn - m_f)
        l_f = al * l_ref[...] + jnp.sum(pn, axis=-1, keepdims=True)
        acc_f = al * acc_ref[...] + jnp.dot(pn.astype(BF16), cn, preferred_element_type=F32)
        _uv_project(acc_f, l_f, wkv_ref, o_ref, DEC_SEQ)


def _sample_attn(page_table_flat, q_lat, q_pe, ckv, kpe, w_ukv_l, cache_ckv, cache_kpe, layer):
    n_pool = cache_ckv.shape[1]
    cc = cache_ckv.reshape(DEPTH * n_pool, PAGE_SIZE, B_KV_RANK)
    ck = jnp.swapaxes(cache_kpe, 2, 3).reshape(DEPTH * n_pool, B_ROPE, PAGE_SIZE)
    rb0 = M_P // DEC_SEQ

    def page_idx(j):
        return lambda b, c, pt: (layer * n_pool + pt[b * N_PAGES + c * SA_PG + j], 0, 0)

    in_specs = [
        pl.BlockSpec((B_HEADS, DEC_SEQ, B_KV_RANK), lambda b, c, pt: (0, rb0 + b, 0)),
        pl.BlockSpec((B_HEADS, DEC_SEQ, B_ROPE), lambda b, c, pt: (0, rb0 + b, 0)),
        pl.BlockSpec((DEC_SEQ, B_KV_RANK), lambda b, c, pt: (rb0 + b, 0)),
        pl.BlockSpec((DEC_SEQ, B_ROPE), lambda b, c, pt: (rb0 + b, 0)),
        pl.BlockSpec((B_KV_RANK, B_HEADS * (B_NOPE + B_V)), lambda b, c, pt: (0, 0)),
    ]
    in_specs += [pl.BlockSpec((None, PAGE_SIZE, B_KV_RANK), page_idx(j)) for j in range(SA_PG)]
    in_specs += [pl.BlockSpec((None, B_ROPE, PAGE_SIZE), page_idx(j)) for j in range(SA_PG)]
    grid_spec = pltpu.PrefetchScalarGridSpec(
        num_scalar_prefetch=1,
        grid=(DEC_BATCH, SA_NCH),
        in_specs=in_specs,
        out_specs=pl.BlockSpec((DEC_SEQ, B_HEADS * B_V), lambda b, c, pt: (b, 0)),
        scratch_shapes=[pltpu.VMEM((SA_ROWS, 1), F32), pltpu.VMEM((SA_ROWS, 1), F32),
                        pltpu.VMEM((SA_ROWS, B_KV_RANK), F32),
                        pltpu.VMEM((SA_PG * PAGE_SIZE, B_KV_RANK), BF16), pltpu.VMEM((B_ROPE, SA_PG * PAGE_SIZE), BF16)],
    )
    return pl.pallas_call(
        _sattn_kernel,
        grid_spec=grid_spec,
        out_shape=jax.ShapeDtypeStruct((M_S, B_HEADS * B_V), F32),
        compiler_params=_cparams("arbitrary", "arbitrary"),
        name="sample_attn",
    )(page_table_flat, q_lat, q_pe, ckv, kpe, w_ukv_l, *([cc] * SA_PG), *([ck] * SA_PG))


def _cmlp_kernel(u_ref, v_ref, nw_ref, wp_ref, ws_ref, bp_ref, bs_ref, o_ref, vn_ref):
    i = pl.program_id(0)
    u = _gelu(u_ref[...])
    v = _gelu(v_ref[...])
    vn = v * lax.rsqrt(jnp.mean(v * v, axis=-1, keepdims=True) + EPS) * nw_ref[...]
    vn_ref[...] = vn
    vb = vn.astype(BF16)

    def run(w_ref, b_ref):
        for g in range(C_GROUPS):
            cols = slice(g * C_GROUP_DIM, (g + 1) * C_GROUP_DIM)
            mixed = jnp.dot(w_ref[g], vb[:, cols], preferred_element_type=F32) + b_ref[g]
            o_ref[:, cols] = u[:, cols] * mixed

    @pl.when(i < NT_P)
    def _():
        run(wp_ref, bp_ref)

    @pl.when(i >= NT_P)
    def _():
        run(ws_ref, bs_ref)


def _block_diag_ws(w_s_l, b_s_l, length):
    reps = TM // length
    wm = jnp.tril(w_s_l[:, :length, :length])
    eye = jnp.eye(reps, dtype=F32)
    wbd = jnp.einsum("ab,gts->gatbs", eye, wm).reshape(C_GROUPS, TM, TM).astype(BF16)
    bcol = jnp.tile(b_s_l[:, :length], (1, reps)).reshape(C_GROUPS, TM, 1)
    return wbd, bcol


def _chunk_mlp(z, c_vnorm_l, w_s_l, b_s_l):
    wp, bp = _block_diag_ws(w_s_l, b_s_l, C_CHUNK)
    ws, bs = _block_diag_ws(w_s_l, b_s_l, DEC_SEQ)
    wspec = pl.BlockSpec((C_GROUPS, TM, TM), lambda i: (0, 0, 0))
    bspec = pl.BlockSpec((C_GROUPS, TM, 1), lambda i: (0, 0, 0))
    row = pl.BlockSpec((TM, C_WIDTH), lambda i: (i, 0))
    return pl.pallas_call(
        _cmlp_kernel,
        grid=(NT,),
        in_specs=[
            pl.BlockSpec((TM, C_WIDTH), lambda i: (i, Z_U // C_WIDTH)),
            pl.BlockSpec((TM, C_WIDTH), lambda i: (i, Z_V // C_WIDTH)),
            pl.BlockSpec((1, C_WIDTH), lambda i: (0, 0)),
            wspec, wspec, bspec, bspec,
        ],
        out_specs=[row, row],
        out_shape=[jax.ShapeDtypeStruct((M_ALL, C_WIDTH), F32), jax.ShapeDtypeStruct((M_ALL, C_WIDTH), F32)],
        compiler_params=_cparams("arbitrary"),
        name="chunk_mlp",
    )(z, z, c_vnorm_l.reshape(1, -1), wp, ws, bp, bs)


def _prep_w_in(w):
    o = 0
    segs = {}
    for name, width in (("xqkv", A_CONV_CH), ("gate", A_V_WIDTH), ("a", A_HEADS), ("bt", A_HEADS), ("cq", B_Q_RANK),
                        ("ckv", B_KV_RANK), ("kpe", B_ROPE), ("u", C_WIDTH), ("v", C_WIDTH)):
        segs[name] = w[:, o:o + width]
        o += width
    zeros = lambda n: jnp.zeros((w.shape[0], n), w.dtype)
    parts = [segs["xqkv"], segs["gate"], segs["u"], segs["v"], segs["ckv"], segs["cq"],
             segs["kpe"], zeros(LANES - B_ROPE), segs["a"], segs["bt"], zeros(LANES - 2 * A_HEADS)]
    out = jnp.concatenate(parts, axis=1).astype(BF16)
    assert out.shape[1] == Z_WIDTH
    return out


def _prep_w_uq(w):
    w3 = w.reshape(B_Q_RANK, B_HEADS, B_NOPE + B_ROPE)
    return jnp.concatenate([w3[:, :, :B_NOPE].reshape(B_Q_RANK, NOPE_W), w3[:, :, B_NOPE:].reshape(B_Q_RANK, ROPE_W)],
                           axis=1)


def _rope_tables():
    half = B_ROPE // 2
    inv = 1.0 / (ROPE_THETA ** (jnp.arange(half, dtype=F32) / half))
    pos = jnp.concatenate([jnp.tile(jnp.arange(SEQ, dtype=jnp.int32), BATCH),
                           jnp.tile(PAST_LEN + jnp.arange(DEC_SEQ, dtype=jnp.int32), DEC_BATCH)])
    ang = pos.astype(F32)[:, None] * inv[None, :]
    cos, sin = jnp.cos(ang), jnp.sin(ang)
    cos_t = jnp.tile(jnp.concatenate([cos, cos], axis=1), (1, B_HEADS))
    sin_t = jnp.tile(jnp.concatenate([-sin, sin], axis=1), (1, B_HEADS))
    return cos_t, sin_t


def _lane_row(vec):
    return jnp.zeros((1, LANES), F32).at[0, :vec.shape[0]].set(vec.astype(F32))


def kernel(x_prompt, x_sample, cache_ckv, cache_kpe, state_gdn, state_conv, page_table, c_prompt, c_sample, w_ada, b_ada, w_in, conv_w, a_log, dt_bias, gdn_norm, q_norm, kv_norm, w_uq, w_ukv, c_vnorm, w_s, b_s, w_out, w_gate, w_up, w_down, w_router, e_gate, e_up, e_down, final_norm):
    x = jnp.concatenate([x_prompt.reshape(M_P, D_MODEL), x_sample.reshape(M_S, D_MODEL)], axis=0)
    c_all = jnp.concatenate([c_prompt, c_sample, jnp.zeros((N_COND_PAD - N_COND, D_MODEL), F32)], axis=0)
    mod = _ada(c_all, w_ada, b_ada)
    cos_t, sin_t = _rope_tables()
    pt_flat = page_table.reshape(-1)

    def mods(l):
        return (mod[l, :BATCH].reshape(BATCH, 6, 1, D_MODEL), mod[l, BATCH:N_COND].reshape(DEC_BATCH, 6, 1, D_MODEL))

    outs = {k: [] for k in ("p_ckv", "p_kpe", "p_gdn", "p_conv", "s_ckv", "s_kpe", "s_gdn", "s_conv", "s_cv")}
    mp, ms = mods(0)
    h = _normmod(x, mp, ms, 0, 1)
    y = None
    for l in range(DEPTH):
        z = _mm(h, _prep_w_in(w_in[l]), 0, tn=1024, name="in_proj")
        st_pad = jnp.concatenate([jnp.zeros((DEC_BATCH, SUBLANES - (A_CONV - 1), A_CONV_CH), F32), state_conv[l]],
                                 axis=1).reshape(M_S, A_CONV_CH)
        qkv = _conv_qkv(z, st_pad, conv_w[l])
        gate_rows = (_lane_row(a_log[l]), _lane_row(dt_bias[l]), gdn_norm[l].reshape(1, A_DV).astype(F32))
        oa_p, sg_p = _gdn(qkv, z, gate_rows, None, row0=0, nseq=BATCH, seq_len=SEQ, c=A_CHUNK, hb=GDN_HB_PROMPT,
                          name="gdn_prompt")
        oa_s, sg_s = _gdn(qkv, z, gate_rows, state_gdn.reshape((-1,) + state_gdn.shape[2:]), row0=M_P,
                          nseq=DEC_BATCH, seq_len=DEC_SEQ, c=DEC_SEQ, hb=A_HEADS, name="gdn_sample",
                          s0_seq0=l * DEC_BATCH)
        q_lat, q_pe, ckv, kpe = _mla_proj(z, cos_t, sin_t, q_norm[l], kv_norm[l], _prep_w_uq(w_uq[l]), w_ukv[l])
        ob_p = _prompt_attn(q_lat, q_pe, ckv, kpe, w_ukv[l])
        ob_s = _sample_attn(pt_flat, q_lat, q_pe, ckv, kpe, w_ukv[l], cache_ckv, cache_kpe, l)
        o_c, vn = _chunk_mlp(z, c_vnorm[l], w_s[l], b_s[l])
        cat = jnp.concatenate([jnp.concatenate([oa_p, oa_s], axis=0), jnp.concatenate([ob_p, ob_s], axis=0), o_c],
                              axis=1).astype(BF16)
        f = _mm(cat, w_out, l, tn=512, name="out_proj")
        x, h = _resid(x, [f], None, mp, ms, 2, nxt=(mp, ms, 3, 4))

        xq = z[:, :A_CONV_CH]
        outs["p_ckv"].append(ckv[:M_P].reshape(BATCH, SEQ, B_KV_RANK))
        outs["p_kpe"].append(kpe[:M_P].reshape(BATCH, SEQ, B_ROPE))
        outs["p_gdn"].append(sg_p)
        outs["p_conv"].append(xq[:M_P].reshape(BATCH, SEQ, A_CONV_CH)[:, SEQ - (A_CONV - 1):])
        outs["s_ckv"].append(ckv[M_P:].reshape(DEC_BATCH, DEC_SEQ, B_KV_RANK))
        outs["s_kpe"].append(kpe[M_P:].reshape(DEC_BATCH, DEC_SEQ, B_ROPE))
        outs["s_gdn"].append(sg_s)
        outs["s_conv"].append(xq[M_P:].reshape(DEC_BATCH, DEC_SEQ, A_CONV_CH)[:, DEC_SEQ - (A_CONV - 1):])
        outs["s_cv"].append(vn[M_P:].reshape(DEC_BATCH, DEC_SEQ, C_WIDTH))

        m_idx = l // 2
        if l % 2 == 0:
            hid = _swiglu_up(h, w_gate.reshape((-1, 1) + w_gate.shape[1:]), w_up.reshape((-1, 1) + w_up.shape[1:]),
                             m_idx, _dense_plan(M_ALL // UP_TM))
            fs = [_swiglu_down(hid, w_down.reshape((-1, 1) + w_down.shape[1:]), m_idx, _dense_plan(M_ALL // DOWN_TM))]
            tw = None
        else:
            w_r = jnp.zeros((D_MODEL, LANES), BF16).at[:, :N_EXPERTS].set(w_router[m_idx].astype(BF16))
            ti, tw = _router(h, w_r)
            src, pos, plan_up, plan_down = _route_plan(ti[:, :TOP_K])
            xs = jnp.take(h, src, axis=0, mode="clip")
            hid = _swiglu_up(xs, e_gate, e_up, m_idx, plan_up)
            ys = _swiglu_down(hid, e_down, m_idx, plan_down)
            fs = [jnp.take(ys, pos[:, k], axis=0, mode="clip") for k in range(TOP_K)]
        if l + 1 < DEPTH:
            mp_n, ms_n = mods(l + 1)
            x, h = _resid(x, fs, tw, mp, ms, 5, nxt=(mp_n, ms_n, 0, 1))
            mp, ms = mp_n, ms_n
        else:
            y = _resid(x, fs, tw, mp, ms, 5, final_w=final_norm)

    st = lambda k: jnp.stack(outs[k])
    return (y[:M_P].reshape(BATCH, SEQ, D_MODEL), y[M_P:].reshape(DEC_BATCH, DEC_SEQ, D_MODEL),
            st("p_ckv"), st("p_kpe"), st("p_gdn"), st("p_conv"),
            st("s_ckv"), st("s_kpe"), st("s_gdn"), st("s_conv"), st("s_cv"))
```

```python
import functools

import numpy as np
import jax
import jax.numpy as jnp
from jax import lax
from jax.experimental import pallas as pl
from jax.experimental.pallas import tpu as pltpu

D_MODEL = 4096
BATCH = 4
SEQ = 2048
DEPTH = 2
DEC_BATCH = 128
DEC_SEQ = 8
PAST_LEN = 16384
PAGE_SIZE = 128
A_HEADS = 12
A_DK = 128
A_DV = 128
A_CONV = 4
A_CHUNK = 64
B_HEADS = 12
B_Q_RANK = 768
B_KV_RANK = 256
B_NOPE = 128
B_ROPE = 64
B_V = 128
ROPE_THETA = 10000.0
C_GROUPS = 8
C_GROUP_DIM = 128
C_CHUNK = 128
D_FF = 14336
N_EXPERTS = 8
TOP_K = 2
EPS = 1e-6

A_QK_WIDTH = A_HEADS * A_DK
A_V_WIDTH = A_HEADS * A_DV
A_CONV_CH = 2 * A_QK_WIDTH + A_V_WIDTH
C_WIDTH = C_GROUPS * C_GROUP_DIM
MIX_WIDTH = A_V_WIDTH + B_HEADS * B_V + C_WIDTH
B_SCALE = (B_NOPE + B_ROPE) ** -0.5
N_PAGES = PAST_LEN // PAGE_SIZE

F32 = jnp.float32
BF16 = jnp.bfloat16
LANES = 128
SUBLANES = 8
VMEM_LIMIT = 56 * 1024 * 1024

M_P = BATCH * SEQ
M_S = DEC_BATCH * DEC_SEQ
M_ALL = M_P + M_S
N_COND = BATCH + DEC_BATCH
N_COND_PAD = -(-N_COND // SUBLANES) * SUBLANES

TM = 512
NT_P = M_P // TM
NT = M_ALL // TM
TE = 256
NE_P = M_P // TE
NE = M_ALL // TE
SEQ_TE = SEQ // TE
SB_TE = TE // DEC_SEQ

Z_XQKV = 0
Z_GATE = A_CONV_CH
Z1_WIDTH = A_CONV_CH + A_V_WIDTH
Z_U = 0
Z_V = 1024
Z_CKV = 2048
Z_CQ = 2304
Z_KPE = 3072
Z_AB = 3200
Z_WIDTH = 3328


def _cparams(*sem):
    return pltpu.CompilerParams(dimension_semantics=sem, vmem_limit_bytes=VMEM_LIMIT)


def _silu(x):
    return x * jax.nn.sigmoid(x)


def _gelu(x):
    return 0.5 * x * (1.0 + lax.erf(x * (2.0 ** -0.5)))


def _cast_rows(dst_ref, src_ref, chunk=256):
    rows = src_ref.shape[0]
    chunk = min(chunk, rows)

    def body(i, c):
        r = pl.multiple_of(i * chunk, chunk)
        dst_ref[pl.ds(r, chunk), :] = src_ref[pl.ds(r, chunk), :].astype(dst_ref.dtype)
        return c

    lax.fori_loop(0, rows // chunk, body, 0)


def _ada_kernel(c_ref, w_ref, b_ref, o_ref):
    a = _silu(c_ref[...]).astype(BF16)
    o_ref[...] = jnp.dot(a, w_ref[...].astype(BF16), preferred_element_type=F32) + b_ref[...]


def _ada(c_all, w_ada, b_ada):
    nl = w_ada.shape[0]
    tn = 512
    n6 = 6 * D_MODEL
    return pl.pallas_call(
        _ada_kernel,
        grid=(nl, n6 // tn),
        in_specs=[
            pl.BlockSpec((N_COND_PAD, D_MODEL), lambda l, n: (0, 0)),
            pl.BlockSpec((None, D_MODEL, tn), lambda l, n: (l, 0, n)),
            pl.BlockSpec((None, 1, tn), lambda l, n: (l, 0, n)),
        ],
        out_specs=pl.BlockSpec((None, N_COND_PAD, tn), lambda l, n: (l, 0, n)),
        out_shape=jax.ShapeDtypeStruct((nl, N_COND_PAD, n6), F32),
        compiler_params=_cparams("arbitrary", "arbitrary"),
        name="ada",
    )(c_all, w_ada, b_ada.reshape(nl, 1, n6))


def _mod_specs(which):
    sp = pl.BlockSpec((1, None, 1, D_MODEL), lambda i: (jnp.minimum(i // SEQ_TE, BATCH - 1), which, 0, 0))
    ss = pl.BlockSpec((SB_TE, None, 1, D_MODEL), lambda i: (jnp.maximum(i - NE_P, 0), which, 0, 0))
    return sp, ss


def _per_group(i, fn):
    @pl.when(i < NE_P)
    def _():
        fn(0)

    @pl.when(i >= NE_P)
    def _():
        fn(1)


def _rms(x3):
    return x3 * lax.rsqrt(jnp.mean(x3 * x3, axis=-1, keepdims=True) + EPS)


def _normmod_kernel(x_ref, shp_ref, shs_ref, scp_ref, scs_ref, h_ref):
    i = pl.program_id(0)

    def run(g):
        ns = (1, SB_TE)[g]
        sh = (shp_ref, shs_ref)[g][...]
        sc = (scp_ref, scs_ref)[g][...]
        x3 = x_ref[...].reshape(ns, TE // ns, D_MODEL)
        h = _rms(x3) * (1.0 + sc) + sh
        h_ref[...] = h.reshape(TE, D_MODEL).astype(h_ref.dtype)

    _per_group(i, run)


def _normmod(x, mod_p, mod_s, i_shift, i_scale):
    shp, shs = _mod_specs(i_shift)
    scp, scs = _mod_specs(i_scale)
    row = pl.BlockSpec((TE, D_MODEL), lambda i: (i, 0))
    return pl.pallas_call(
        _normmod_kernel,
        grid=(NE,),
        in_specs=[row, shp, shs, scp, scs],
        out_specs=row,
        out_shape=jax.ShapeDtypeStruct((M_ALL, D_MODEL), BF16),
        compiler_params=_cparams("arbitrary"),
        name="normmod",
    )(x, mod_p, mod_s, mod_p, mod_s)


def _resid_kernel(*refs, n_f, mode):
    i = pl.program_id(0)
    x_ref, f0_ref = refs[0], refs[1]
    k = 2
    if n_f == 2:
        f1_ref, tw_ref = refs[2], refs[3]
        k = 4
    gp_ref, gs_ref = refs[k], refs[k + 1]
    if mode == "mod":
        shp_ref, shs_ref, scp_ref, scs_ref, y_ref, h_ref = refs[k + 2:k + 8]
    else:
        fw_ref, outp_ref, outs_ref = refs[k + 2:k + 5]

    def run(g):
        ns = (1, SB_TE)[g]
        f = f0_ref[...]
        if n_f == 2:
            tw = tw_ref[...]
            f = f * tw[:, 0:1] + f1_ref[...] * tw[:, 1:2]
        shape3 = (ns, TE // ns, D_MODEL)
        gate = (gp_ref, gs_ref)[g][...]
        y3 = x_ref[...].reshape(shape3) + gate * f.reshape(shape3)
        if mode == "mod":
            sh = (shp_ref, shs_ref)[g][...]
            sc = (scp_ref, scs_ref)[g][...]
            y_ref[...] = y3.reshape(TE, D_MODEL)
            h_ref[...] = (_rms(y3) * (1.0 + sc) + sh).reshape(TE, D_MODEL).astype(h_ref.dtype)
        else:
            (outp_ref, outs_ref)[g][...] = (_rms(y3) * fw_ref[...]).reshape(TE, D_MODEL)

    _per_group(i, run)


def _resid(x, fs, tw, mod_p, mod_s, i_gate, nxt=None, final_w=None):
    row = pl.BlockSpec((TE, D_MODEL), lambda i: (i, 0))
    n_f = len(fs)
    args = [x] + list(fs)
    specs = [row] * (1 + n_f)
    if n_f == 2:
        args.append(tw)
        specs.append(pl.BlockSpec((TE, LANES), lambda i: (i, 0)))
    gp, gs = _mod_specs(i_gate)
    args += [mod_p, mod_s]
    specs += [gp, gs]
    if nxt is not None:
        np_, ns_, i_shift, i_scale = nxt
        shp, shs = _mod_specs(i_shift)
        scp, scs = _mod_specs(i_scale)
        args += [np_, ns_, np_, ns_]
        specs += [shp, shs, scp, scs]
        out_specs = [row, row]
        out_shape = [jax.ShapeDtypeStruct((M_ALL, D_MODEL), F32), jax.ShapeDtypeStruct((M_ALL, D_MODEL), BF16)]
        mode = "mod"
    else:
        args.append(final_w.reshape(1, 1, D_MODEL))
        specs.append(pl.BlockSpec((1, 1, D_MODEL), lambda i: (0, 0, 0)))
        out_specs = [pl.BlockSpec((TE, D_MODEL), lambda i: (jnp.minimum(i, NE_P - 1), 0)),
                     pl.BlockSpec((TE, D_MODEL), lambda i: (jnp.maximum(i - NE_P, 0), 0))]
        out_shape = [jax.ShapeDtypeStruct((M_P, D_MODEL), F32), jax.ShapeDtypeStruct((M_S, D_MODEL), F32)]
        mode = "final"
    return pl.pallas_call(
        functools.partial(_resid_kernel, n_f=n_f, mode=mode),
        grid=(NE,),
        in_specs=specs,
        out_specs=out_specs,
        out_shape=out_shape,
        compiler_params=_cparams("arbitrary"),
        name="resid_" + mode,
    )(*args)


def _mm_kernel(x_ref, w_ref, o_ref, *scratch, cast_w):
    if cast_w:
        wb_ref, = scratch

        @pl.when(pl.program_id(1) == 0)
        def _():
            _cast_rows(wb_ref, w_ref)

        w = wb_ref[...]
    else:
        w = w_ref[...]
    o_ref[...] = jnp.dot(x_ref[...], w, preferred_element_type=F32).astype(o_ref.dtype)


def _mm(x, w, layer, tn, out_dtype=F32, name="mm", n_cols=None):
    m, k = x.shape
    n = w.shape[-1] if n_cols is None else n_cols
    cast_w = w.dtype != BF16
    if w.ndim == 3:
        w_spec = pl.BlockSpec((None, k, tn), lambda j, i: (layer, 0, j))
    else:
        w_spec = pl.BlockSpec((k, tn), lambda j, i: (0, j))
    scratch = [pltpu.VMEM((k, tn), BF16)] if cast_w else []
    return pl.pallas_call(
        functools.partial(_mm_kernel, cast_w=cast_w),
        grid=(pl.cdiv(n, tn), m // TM),
        in_specs=[pl.BlockSpec((TM, k), lambda j, i: (i, 0)), w_spec],
        out_specs=pl.BlockSpec((TM, tn), lambda j, i: (i, j)),
        out_shape=jax.ShapeDtypeStruct((m, n), out_dtype),
        scratch_shapes=scratch,
        compiler_params=_cparams("arbitrary", "arbitrary"),
        name=name,
    )(x, w)


UP_TM = 512
DOWN_TM = 1024
DOWN_SUB = 512
assert M_ALL % DOWN_TM == 0


def _swiglu_kernel(plan_ref, x_ref, wg_ref, wu_ref, o_ref, wgb_ref, wub_ref):
    i = pl.program_id(1)
    e_prev = plan_ref[0, jnp.maximum(i - 1, 0)]
    changed = jnp.logical_or(i == 0, plan_ref[0, i] != e_prev)
    used = plan_ref[1, i] > 0

    @pl.when(changed)
    def _():
        _cast_rows(wgb_ref, wg_ref)
        _cast_rows(wub_ref, wu_ref)

    @pl.when(used)
    def _():
        x = x_ref[...]
        g = jnp.dot(x, wgb_ref[...], preferred_element_type=F32)
        u = jnp.dot(x, wub_ref[...], preferred_element_type=F32)
        o_ref[...] = (_silu(g) * u).astype(o_ref.dtype)

    @pl.when(jnp.logical_not(used))
    def _():
        o_ref[...] = jnp.zeros_like(o_ref)


def _swiglu_up(x, w_gate, w_up, layer, plan, tn=512):
    m = x.shape[0]
    w_spec = pl.BlockSpec((None, None, D_MODEL, tn), lambda j, i, pr: (layer, pr[0, i], 0, j))
    grid_spec = pltpu.PrefetchScalarGridSpec(
        num_scalar_prefetch=1,
        grid=(D_FF // tn, m // UP_TM),
        in_specs=[pl.BlockSpec((UP_TM, D_MODEL), lambda j, i, pr: (pr[2, i], 0)), w_spec, w_spec],
        out_specs=pl.BlockSpec((UP_TM, tn), lambda j, i, pr: (i, j)),
        scratch_shapes=[pltpu.VMEM((D_MODEL, tn), BF16), pltpu.VMEM((D_MODEL, tn), BF16)],
    )
    return pl.pallas_call(
        _swiglu_kernel,
        grid_spec=grid_spec,
        out_shape=jax.ShapeDtypeStruct((m, D_FF), BF16),
        compiler_params=_cparams("arbitrary", "arbitrary"),
        name="swiglu_up",
    )(plan, x, w_gate, w_up)


def _down_kernel(plan_ref, x_ref, w_ref, o_ref):
    i = pl.program_id(0)
    kk = pl.program_id(2)
    nsub = plan_ref[1, i]

    def accumulate(r0):
        p = jnp.dot(x_ref[r0:, :], w_ref[...].astype(BF16), preferred_element_type=F32)

        @pl.when(kk == 0)
        def _():
            o_ref[r0:, :] = p

        @pl.when(kk > 0)
        def _():
            o_ref[r0:, :] += p

    @pl.when(nsub == 2)
    def _():
        accumulate(0)

    @pl.when(nsub == 1)
    def _():
        accumulate(DOWN_TM - DOWN_SUB)

    @pl.when(jnp.logical_and(kk == 0, nsub < 2))
    def _():
        o_ref[0:DOWN_TM - DOWN_SUB, :] = jnp.zeros((DOWN_TM - DOWN_SUB, o_ref.shape[1]), o_ref.dtype)

    @pl.when(jnp.logical_and(kk == 0, nsub < 1))
    def _():
        o_ref[DOWN_TM - DOWN_SUB:, :] = jnp.zeros((DOWN_SUB, o_ref.shape[1]), o_ref.dtype)


def _swiglu_down(hid, w_down, layer, plan, tn=1024, tk=2048):
    m = hid.shape[0]
    grid_spec = pltpu.PrefetchScalarGridSpec(
        num_scalar_prefetch=1,
        grid=(m // DOWN_TM, D_MODEL // tn, D_FF // tk),
        in_specs=[
            pl.BlockSpec((DOWN_TM, tk), lambda i, j, k, pr: (pr[2, i], k)),
            pl.BlockSpec((None, None, tk, tn), lambda i, j, k, pr: (layer, pr[0, i], k, j)),
        ],
        out_specs=pl.BlockSpec((DOWN_TM, tn), lambda i, j, k, pr: (i, j)),
    )
    return pl.pallas_call(
        _down_kernel,
        grid_spec=grid_spec,
        out_shape=jax.ShapeDtypeStruct((m, D_MODEL), F32),
        compiler_params=_cparams("arbitrary", "arbitrary", "arbitrary"),
        name="swiglu_down",
    )(plan, hid, w_down)


def _dense_plan(tiles):
    t = jnp.arange(tiles, dtype=jnp.int32)
    return jnp.stack([jnp.zeros_like(t), jnp.full_like(t, 2), t])


def _router_kernel(h_ref, w_ref, ti_ref, tw_ref):
    logits = jnp.dot(h_ref[...], w_ref[...], preferred_element_type=F32)
    lane = lax.broadcasted_iota(jnp.int32, logits.shape, 1)
    neg = jnp.float32(-jnp.inf)
    logits = jnp.where(lane < N_EXPERTS, logits, neg)
    m1 = jnp.max(logits, axis=-1, keepdims=True)
    i1 = jnp.min(jnp.where(logits == m1, lane, LANES), axis=-1, keepdims=True)
    rest = jnp.where(lane == i1, neg, logits)
    m2 = jnp.max(rest, axis=-1, keepdims=True)
    i2 = jnp.min(jnp.where(rest == m2, lane, LANES), axis=-1, keepdims=True)
    e2 = jnp.exp(m2 - m1)
    den = 1.0 + e2
    ti_ref[...] = jnp.where(lane == 0, i1, jnp.where(lane == 1, i2, 0))
    tw_ref[...] = jnp.where(lane == 0, 1.0 / den, jnp.where(lane == 1, e2 / den, 0.0))


def _router(h, w_router_pad):
    row = pl.BlockSpec((TM, LANES), lambda i: (i, 0))
    return pl.pallas_call(
        _router_kernel,
        grid=(NT,),
        in_specs=[pl.BlockSpec((TM, D_MODEL), lambda i: (i, 0)), pl.BlockSpec((D_MODEL, LANES), lambda i: (0, 0))],
        out_specs=[row, row],
        out_shape=[jax.ShapeDtypeStruct((M_ALL, LANES), jnp.int32), jax.ShapeDtypeStruct((M_ALL, LANES), F32)],
        compiler_params=_cparams("arbitrary"),
        name="router",
    )(h, w_router_pad)


MOE_ROWS = M_ALL * TOP_K + N_EXPERTS * DOWN_TM


def _tile_plan(lo, ends, tm, sub):
    tiles = MOE_ROWS // tm
    idx = jnp.arange(tiles, dtype=jnp.int32)
    t0 = idx * tm
    e = jnp.minimum(jnp.sum((t0[:, None] >= ends[None, :]).astype(jnp.int32), axis=1), N_EXPERTS - 1)
    used_rows = jnp.where(t0 < ends[e], jnp.clip(t0 + tm - lo[e], 0, tm), 0)
    nsub = (used_rows + sub - 1) // sub
    fetch = lax.cummax(jnp.where(nsub > 0, idx, 0), axis=0)
    return jnp.stack([e, nsub, fetch]).astype(jnp.int32)


def _route_plan(ti):
    e_flat = ti.reshape(-1)
    onehot = (e_flat[:, None] == jnp.arange(N_EXPERTS, dtype=jnp.int32)[None, :]).astype(jnp.int32)
    counts = jnp.sum(onehot, axis=0)
    rank = jnp.sum((jnp.cumsum(onehot, axis=0) - onehot) * onehot, axis=1)
    padded = ((counts + DOWN_TM - 1) // DOWN_TM) * DOWN_TM
    ends = jnp.cumsum(padded)
    lo = ends - counts
    pos = lo[e_flat] + rank
    src = jnp.zeros((MOE_ROWS,), jnp.int32).at[pos].set(jnp.arange(e_flat.shape[0], dtype=jnp.int32) // TOP_K)
    return (src, pos.reshape(-1, TOP_K), _tile_plan(lo, ends, UP_TM, UP_TM), _tile_plan(lo, ends, DOWN_TM, DOWN_SUB))


def _conv_kernel(x_ref, hp_ref, hs_ref, w_ref, o_ref, sp_ref, ss_ref):
    i = pl.program_id(0)
    part = pl.program_id(1)
    width = x_ref.shape[1]
    w = w_ref[...]

    def finish(y):
        y = _silu(y)
        qk = part < 2
        post = jnp.where(part == 0, A_DK ** -0.5, 1.0)
        for h in range(width // A_DK):
            seg = y[:, h * A_DK:(h + 1) * A_DK]
            ss = jnp.sum(seg * seg, axis=-1, keepdims=True)
            scale = jnp.where(qk, lax.rsqrt(ss + EPS), 1.0) * post
            o_ref[:, h * A_DK:(h + 1) * A_DK] = seg * scale

    @pl.when(i < NE_P)
    def _():
        first = (i % SEQ_TE) == 0
        sp_ref[0:SUBLANES, :] = jnp.where(first, 0.0, hp_ref[...])
        sp_ref[SUBLANES:, :] = x_ref[...]
        acc = sp_ref[SUBLANES - 3:SUBLANES - 3 + TE, :] * w[0:1, :]
        for j in range(1, A_CONV):
            acc = acc + sp_ref[SUBLANES - 3 + j:SUBLANES - 3 + j + TE, :] * w[j:j + 1, :]
        finish(acc)

    @pl.when(i >= NE_P)
    def _():
        ss_ref[:, 0:SUBLANES, :] = hs_ref[...].reshape(SB_TE, SUBLANES, width)
        ss_ref[:, SUBLANES:, :] = x_ref[...].reshape(SB_TE, DEC_SEQ, width)
        acc = ss_ref[:, SUBLANES - 3:SUBLANES - 3 + DEC_SEQ, :] * w[0:1, :]
        for j in range(1, A_CONV):
            acc = acc + ss_ref[:, SUBLANES - 3 + j:SUBLANES - 3 + j + DEC_SEQ, :] * w[j:j + 1, :]
        finish(acc.reshape(TE, width))


def _conv_qkv(z, conv_state_pad, conv_w_l):
    width = A_QK_WIDTH
    assert DEC_SEQ == SUBLANES
    return pl.pallas_call(
        _conv_kernel,
        grid=(NE, 3),
        in_specs=[
            pl.BlockSpec((TE, width), lambda i, p: (i, p)),
            pl.BlockSpec((SUBLANES, width), lambda i, p: (jnp.maximum(i * (TE // SUBLANES) - 1, 0), p)),
            pl.BlockSpec((TE, width), lambda i, p: (jnp.maximum(i - NE_P, 0), p)),
            pl.BlockSpec((A_CONV, width), lambda i, p: (0, p)),
        ],
        out_specs=pl.BlockSpec((TE, width), lambda i, p: (i, p)),
        out_shape=jax.ShapeDtypeStruct((M_ALL, A_CONV_CH), F32),
        scratch_shapes=[pltpu.VMEM((TE + SUBLANES, width), F32), pltpu.VMEM((SB_TE, 2 * SUBLANES, width), F32)],
        compiler_params=_cparams("arbitrary", "arbitrary"),
        name="conv_qkv",
    )(z, z, conv_state_pad, conv_w_l)


GDN_HB_PROMPT = A_HEADS

_DN_NN = (((1,), (0,)), ((), ()))
_DN_NT = (((1,), (1,)), ((), ()))
_DN_TN = (((0,), (0,)), ((), ()))


def _dotp(a, b, prec, dn=_DN_NN):
    if prec == "f32":
        return lax.dot_general(a, b, dn, preferred_element_type=F32, precision=lax.Precision.HIGHEST)
    a_hi, b_hi = a.astype(BF16), b.astype(BF16)
    out = lax.dot_general(a_hi, b_hi, dn, preferred_element_type=F32)
    if prec == "bf16x3":
        a_lo = (a - a_hi.astype(F32)).astype(BF16)
        b_lo = (b - b_hi.astype(F32)).astype(BF16)
        out = out + (lax.dot_general(a_hi, b_lo, dn, preferred_element_type=F32)
                     + lax.dot_general(a_lo, b_hi, dn, preferred_element_type=F32))
    return out


def _gdn_kernel(*refs, c, hb, has_s0, p_inv, p_bulk):
    if has_s0:
        q_ref, k_ref, v_ref, zg_ref, ab_ref, al_ref, dt_ref, nw_ref, s0_ref, o_ref, sf_ref, s_ref = refs
    else:
        q_ref, k_ref, v_ref, zg_ref, ab_ref, al_ref, dt_ref, nw_ref, o_ref, sf_ref, s_ref = refs
    hblk = pl.program_id(1)
    ch = pl.program_id(2)

    @pl.when(ch == 0)
    def _():
        if has_s0:
            s_ref[...] = s0_ref[...]
        else:
            s_ref[...] = jnp.zeros_like(s_ref)

    ab = ab_ref[...]
    x = ab + dt_ref[...]
    softplus = jnp.maximum(x, 0.0) + jnp.log1p(jnp.exp(-jnp.abs(x)))
    g_all = -jnp.exp(al_ref[...]) * softplus
    beta_all = jax.nn.sigmoid(ab)
    r_i = lax.broadcasted_iota(jnp.int32, (c, c), 0)
    c_i = lax.broadcasted_iota(jnp.int32, (c, c), 1)
    incl = r_i >= c_i
    strict = r_i > c_i
    eye = (r_i == c_i).astype(F32)
    gc_all = _dotp(incl.astype(F32), g_all, "f32")
    sel = (lax.broadcasted_iota(jnp.int32, (2 * SUBLANES, LANES), 0)
           == lax.broadcasted_iota(jnp.int32, (2 * SUBLANES, LANES), 1)).astype(F32)
    gct_all = _dotp(sel, gc_all, "f32", _DN_NT)
    lane = lax.broadcasted_iota(jnp.int32, (c, LANES), 1)
    sub = lax.broadcasted_iota(jnp.int32, (2 * SUBLANES, c), 0)
    nw = nw_ref[...]

    hs = range(hb)
    cols = [slice(j * A_DK, (j + 1) * A_DK) for j in hs]
    heads = [hblk * hb + j for j in hs]
    q = [q_ref[:, cols[j]] for j in hs]
    k = [k_ref[:, cols[j]] for j in hs]
    v = [v_ref[:, cols[j]] for j in hs]
    s_old = [s_ref[j] for j in hs]
    gc = [jnp.sum(jnp.where(lane == heads[j], gc_all, 0.0), axis=1, keepdims=True) for j in hs]
    beta = [jnp.sum(jnp.where(lane == heads[j] + A_HEADS, beta_all, 0.0), axis=1, keepdims=True) for j in hs]
    gc_row = [jnp.sum(jnp.where(sub == heads[j], gct_all, 0.0), axis=0, keepdims=True) for j in hs]
    gc_last = [gc[j][c - 1:c, :] for j in hs]
    decay = [jnp.where(incl, jnp.exp(jnp.where(incl, gc[j] - gc_row[j], 0.0)), 0.0) for j in hs]
    kb = [k[j] * beta[j] for j in hs]
    x_pow = [-jnp.where(strict, _dotp(kb[j], k[j], p_bulk, _DN_NT) * decay[j], 0.0) for j in hs]
    qk = [_dotp(q[j], k[j], p_bulk, _DN_NT) * decay[j] for j in hs]
    t_inv = [eye + x_pow[j] for j in hs]
    for _ in range(max(int(np.ceil(np.log2(c))) - 1, 0)):
        x_pow = [_dotp(x_pow[j], x_pow[j], p_inv) for j in hs]
        t_inv = [t_inv[j] + _dotp(t_inv[j], x_pow[j], p_inv) for j in hs]
    eg = [jnp.exp(gc[j]) for j in hs]
    wu = [_dotp(t_inv[j], jnp.concatenate([kb[j] * eg[j], v[j] * beta[j]], axis=1), p_bulk) for j in hs]
    qs = [_dotp(q[j] * eg[j], s_old[j], p_bulk) for j in hs]
    v_new = [wu[j][:, A_DK:] - _dotp(wu[j][:, :A_DK], s_old[j], p_bulk) for j in hs]
    o = [qs[j] + _dotp(qk[j], v_new[j], p_bulk) for j in hs]
    kv = [_dotp(k[j] * jnp.exp(gc_last[j] - gc[j]), v_new[j], p_bulk, _DN_TN) for j in hs]
    for j in hs:
        s_ref[j] = s_old[j] * jnp.exp(gc_last[j]) + kv[j]
        o_n = o[j] * lax.rsqrt(jnp.mean(o[j] * o[j], axis=-1, keepdims=True) + EPS) * nw
        o_ref[:, cols[j]] = o_n * _silu(zg_ref[:, cols[j]])

    @pl.when(ch == pl.num_programs(2) - 1)
    def _():
        sf_ref[...] = s_ref[...]


def _gdn(qkv, z1, z2, gate_rows, s0, *, row0, nseq, seq_len, c, hb, name, s0_seq0=0, p_inv="bf16x3", p_bulk="bf16"):
    al_row, dt_row, nw_row = gate_rows
    nch = seq_len // c
    wblk = hb * A_DK
    rb0 = row0 // c
    nq = A_QK_WIDTH // wblk

    def rows(b, h, n):
        return rb0 + b * nch + n

    in_specs = [
        pl.BlockSpec((c, wblk), lambda b, h, n: (rows(b, h, n), h)),
        pl.BlockSpec((c, wblk), lambda b, h, n: (rows(b, h, n), nq + h)),
        pl.BlockSpec((c, wblk), lambda b, h, n: (rows(b, h, n), 2 * nq + h)),
        pl.BlockSpec((c, wblk), lambda b, h, n: (rows(b, h, n), Z_GATE // wblk + h)),
        pl.BlockSpec((c, LANES), lambda b, h, n: (rows(b, h, n), Z_AB // LANES)),
        pl.BlockSpec((1, LANES), lambda b, h, n: (0, 0)),
        pl.BlockSpec((1, LANES), lambda b, h, n: (0, 0)),
        pl.BlockSpec((1, LANES), lambda b, h, n: (0, 0)),
    ]
    args = [qkv, qkv, qkv, z1, z2, al_row, dt_row, nw_row]
    st_spec = pl.BlockSpec((None, hb, A_DK, A_DV), lambda b, h, n: (b, h, 0, 0))
    if s0 is not None:
        in_specs.append(pl.BlockSpec((None, hb, A_DK, A_DV), lambda b, h, n: (s0_seq0 + b, h, 0, 0)))
        args.append(s0)
    return pl.pallas_call(
        functools.partial(_gdn_kernel, c=c, hb=hb, has_s0=s0 is not None, p_inv=p_inv, p_bulk=p_bulk),
        grid=(nseq, A_HEADS // hb, nch),
        in_specs=in_specs,
        out_specs=[pl.BlockSpec((c, wblk), lambda b, h, n: (b * nch + n, h)), st_spec],
        out_shape=[jax.ShapeDtypeStruct((nseq * seq_len, A_V_WIDTH), F32),
                   jax.ShapeDtypeStruct((nseq, A_HEADS, A_DK, A_DV), F32)],
        scratch_shapes=[pltpu.VMEM((hb, A_DK, A_DV), F32)],
        compiler_params=_cparams("arbitrary", "arbitrary", "arbitrary"),
        name=name,
    )(*args)


ROPE_W = B_HEADS * B_ROPE
NOPE_W = B_HEADS * B_NOPE


def _rope_lanes(x, cos, sin_signed):
    w = x.shape[-1]
    half = B_ROPE // 2
    lane = lax.broadcasted_iota(jnp.int32, x.shape, 1)
    partner = jnp.where((lane % B_ROPE) < half, pltpu.roll(x, w - half, 1), pltpu.roll(x, half, 1))
    return x * cos + partner * sin_signed


def _mla_proj_kernel(cq_ref, ckv_ref, kpe_ref, cos_ref, sin_ref, qn_ref, kvn_ref, wq_ref, wkv_ref,
                     ql_ref, qp_ref, ckvo_ref, kpeo_ref, wqb_ref, wkvb_ref):
    @pl.when(pl.program_id(0) == 0)
    def _():
        _cast_rows(wqb_ref, wq_ref)
        _cast_rows(wkvb_ref, wkv_ref)

    cq = cq_ref[...]
    cqn = cq * lax.rsqrt(jnp.mean(cq * cq, axis=-1, keepdims=True) + EPS) * qn_ref[...]
    q = jnp.dot(cqn.astype(BF16), wqb_ref[...], preferred_element_type=F32)
    cos = cos_ref[...]
    sin = sin_ref[...]
    q_pe = _rope_lanes(q[:, NOPE_W:], cos, sin)
    for h in range(B_HEADS):
        q_nope = q[:, h * B_NOPE:(h + 1) * B_NOPE].astype(BF16)
        w_uk = wkvb_ref[:, h * (B_NOPE + B_V):h * (B_NOPE + B_V) + B_NOPE]
        ql_ref[h] = lax.dot_general(q_nope, w_uk, (((1,), (1,)), ((), ())),
                                    preferred_element_type=F32).astype(ql_ref.dtype)
        qp_ref[h] = q_pe[:, h * B_ROPE:(h + 1) * B_ROPE].astype(qp_ref.dtype)
    ckv = ckv_ref[...]
    ckvo_ref[...] = ckv * lax.rsqrt(jnp.mean(ckv * ckv, axis=-1, keepdims=True) + EPS) * kvn_ref[...]
    kpe = _rope_lanes(kpe_ref[...], cos[:, :LANES], sin[:, :LANES])
    kpeo_ref[...] = kpe[:, :B_ROPE]


def _mla_proj(z, cos_t, sin_t, q_norm_l, kv_norm_l, w_uq_perm, w_ukv_l):
    full = lambda shape: pl.BlockSpec(shape, lambda i: (0,) * len(shape))
    return pl.pallas_call(
        _mla_proj_kernel,
        grid=(NT,),
        in_specs=[
            pl.BlockSpec((TM, B_Q_RANK), lambda i: (i, Z_CQ // B_Q_RANK)),
            pl.BlockSpec((TM, B_KV_RANK), lambda i: (i, Z_CKV // B_KV_RANK)),
            pl.BlockSpec((TM, LANES), lambda i: (i, Z_KPE // LANES)),
            pl.BlockSpec((TM, ROPE_W), lambda i: (i, 0)),
            pl.BlockSpec((TM, ROPE_W), lambda i: (i, 0)),
            full((1, B_Q_RANK)), full((1, B_KV_RANK)),
            full((B_Q_RANK, NOPE_W + ROPE_W)), full((B_KV_RANK, B_HEADS * (B_NOPE + B_V))),
        ],
        out_specs=[
            pl.BlockSpec((B_HEADS, TM, B_KV_RANK), lambda i: (0, i, 0)),
            pl.BlockSpec((B_HEADS, TM, B_ROPE), lambda i: (0, i, 0)),
            pl.BlockSpec((TM, B_KV_RANK), lambda i: (i, 0)),
            pl.BlockSpec((TM, B_ROPE), lambda i: (i, 0)),
        ],
        out_shape=[
            jax.ShapeDtypeStruct((B_HEADS, M_ALL, B_KV_RANK), F32),
            jax.ShapeDtypeStruct((B_HEADS, M_ALL, B_ROPE), F32),
            jax.ShapeDtypeStruct((M_ALL, B_KV_RANK), F32),
            jax.ShapeDtypeStruct((M_ALL, B_ROPE), F32),
        ],
        scratch_shapes=[pltpu.VMEM((B_Q_RANK, NOPE_W + ROPE_W), BF16),
                        pltpu.VMEM((B_KV_RANK, B_HEADS * (B_NOPE + B_V)), BF16)],
        compiler_params=_cparams("arbitrary"),
        name="mla_proj",
    )(z, z, z, cos_t, sin_t, q_norm_l.reshape(1, -1), kv_norm_l.reshape(1, -1), w_uq_perm, w_ukv_l)


def _uv_project(acc, l, wkv_ref, o_ref, rows):
    o_lat = acc / l
    for h in range(B_HEADS):
        lo = h * (B_NOPE + B_V) + B_NOPE
        w_uv = wkv_ref[:, lo:lo + B_V].astype(BF16)
        o_h = o_lat[h * rows:(h + 1) * rows, :].astype(BF16)
        o_ref[:, h * B_V:(h + 1) * B_V] = jnp.dot(o_h, w_uv, preferred_element_type=F32)


PA_TQ = 128
PA_TK = 512


def _pattn_kernel(ql_ref, qp_ref, ckv_ref, kpe_ref, wkv_ref, o_ref, m_ref, l_ref, acc_ref):
    qb = pl.program_id(1)
    kb = pl.program_id(2)
    k_last = (qb * PA_TQ + PA_TQ - 1) // PA_TK
    rows = B_HEADS * PA_TQ

    @pl.when(kb == 0)
    def _():
        m_ref[...] = jnp.full_like(m_ref, -jnp.inf)
        l_ref[...] = jnp.zeros_like(l_ref)
        acc_ref[...] = jnp.zeros_like(acc_ref)

    @pl.when(kb <= k_last)
    def _():
        ql = ql_ref[...].reshape(rows, B_KV_RANK).astype(BF16)
        qp = qp_ref[...].reshape(rows, B_ROPE).astype(BF16)
        ckv = ckv_ref[...].astype(BF16)
        kpe = kpe_ref[...].astype(BF16)
        dn = (((1,), (1,)), ((), ()))
        s = (lax.dot_general(ql, ckv, dn, preferred_element_type=F32)
             + lax.dot_general(qp, kpe, dn, preferred_element_type=F32)) * B_SCALE
        qpos = qb * PA_TQ + lax.broadcasted_iota(jnp.int32, (B_HEADS, PA_TQ, PA_TK), 1).reshape(rows, PA_TK)
        kpos = kb * PA_TK + lax.broadcasted_iota(jnp.int32, (rows, PA_TK), 1)
        s = jnp.where(kpos <= qpos, s, -jnp.inf)
        m_old = m_ref[...]
        m_new = jnp.maximum(m_old, jnp.max(s, axis=-1, keepdims=True))
        alpha = jnp.exp(m_old - m_new)
        p = jnp.exp(s - m_new)
        l_ref[...] = alpha * l_ref[...] + jnp.sum(p, axis=-1, keepdims=True)
        acc_ref[...] = alpha * acc_ref[...] + jnp.dot(p.astype(BF16), ckv, preferred_element_type=F32)
        m_ref[...] = m_new

    @pl.when(kb == k_last)
    def _():
        _uv_project(acc_ref[...], l_ref[...], wkv_ref, o_ref, PA_TQ)


def _prompt_attn(q_lat, q_pe, ckv, kpe, w_ukv_l):
    nqb = SEQ // PA_TQ
    nkb = SEQ // PA_TK
    rows = B_HEADS * PA_TQ

    def kv_idx(b, qb, kb):
        return (b * nkb + jnp.minimum(kb, (qb * PA_TQ + PA_TQ - 1) // PA_TK), 0)

    return pl.pallas_call(
        _pattn_kernel,
        grid=(BATCH, nqb, nkb),
        in_specs=[
            pl.BlockSpec((B_HEADS, PA_TQ, B_KV_RANK), lambda b, qb, kb: (0, b * nqb + qb, 0)),
            pl.BlockSpec((B_HEADS, PA_TQ, B_ROPE), lambda b, qb, kb: (0, b * nqb + qb, 0)),
            pl.BlockSpec((PA_TK, B_KV_RANK), kv_idx),
            pl.BlockSpec((PA_TK, B_ROPE), kv_idx),
            pl.BlockSpec((B_KV_RANK, B_HEADS * (B_NOPE + B_V)), lambda b, qb, kb: (0, 0)),
        ],
        out_specs=pl.BlockSpec((PA_TQ, B_HEADS * B_V), lambda b, qb, kb: (b * nqb + qb, 0)),
        out_shape=jax.ShapeDtypeStruct((M_P, B_HEADS * B_V), F32),
        scratch_shapes=[pltpu.VMEM((rows, 1), F32), pltpu.VMEM((rows, 1), F32), pltpu.VMEM((rows, B_KV_RANK), F32)],
        compiler_params=_cparams("arbitrary", "arbitrary", "arbitrary"),
        name="prompt_attn",
    )(q_lat, q_pe, ckv, kpe, w_ukv_l)


SA_PG = 32
SA_NCH = N_PAGES // SA_PG
SA_ROWS = B_HEADS * DEC_SEQ


def _sattn_kernel(pt_ref, *refs):
    ql_ref, qp_ref, cn_ref, kn_ref, wkv_ref = refs[:5]
    ckv_refs = refs[5:5 + SA_PG]
    kpe_refs = refs[5 + SA_PG:5 + 2 * SA_PG]
    o_ref, m_ref, l_ref, acc_ref, kc_ref, kp_ref = refs[5 + 2 * SA_PG:]
    ch = pl.program_id(1)
    dn = (((1,), (1,)), ((), ()))
    ql = ql_ref[...].reshape(SA_ROWS, B_KV_RANK).astype(BF16)
    qp = qp_ref[...].reshape(SA_ROWS, B_ROPE).astype(BF16)

    @pl.when(ch == 0)
    def _():
        m_ref[...] = jnp.full_like(m_ref, -jnp.inf)
        l_ref[...] = jnp.zeros_like(l_ref)
        acc_ref[...] = jnp.zeros_like(acc_ref)

    for j in range(SA_PG):
        kc_ref[j * PAGE_SIZE:(j + 1) * PAGE_SIZE, :] = ckv_refs[j][...].astype(BF16)
        kp_ref[:, j * PAGE_SIZE:(j + 1) * PAGE_SIZE] = kpe_refs[j][...].astype(BF16)
    kc = kc_ref[...]
    s = (lax.dot_general(ql, kc, dn, preferred_element_type=F32)
         + jnp.dot(qp, kp_ref[...], preferred_element_type=F32)) * B_SCALE
    m_old = m_ref[...]
    m_new = jnp.maximum(m_old, jnp.max(s, axis=-1, keepdims=True))
    alpha = jnp.exp(m_old - m_new)
    p = jnp.exp(s - m_new)
    l_ref[...] = alpha * l_ref[...] + jnp.sum(p, axis=-1, keepdims=True)
    acc_ref[...] = alpha * acc_ref[...] + jnp.dot(p.astype(BF16), kc, preferred_element_type=F32)
    m_ref[...] = m_new

    @pl.when(ch == pl.num_programs(1) - 1)
    def _():
        cn = cn_ref[...].astype(BF16)
        kn = kn_ref[...].astype(BF16)
        sn = (lax.dot_general(ql, cn, dn, preferred_element_type=F32)
              + lax.dot_general(qp, kn, dn, preferred_element_type=F32)) * B_SCALE
        t_q = lax.broadcasted_iota(jnp.int32, (B_HEADS, DEC_SEQ, DEC_SEQ), 1).reshape(SA_ROWS, DEC_SEQ)
        t_k = lax.broadcasted_iota(jnp.int32, (SA_ROWS, DEC_SEQ), 1)
        sn = jnp.where(t_k <= t_q, sn, -jnp.inf)
        m_o = m_ref[...]
        m_f = jnp.maximum(m_o, jnp.max(sn, axis=-1, keepdims=True))
        al = jnp.exp(m_o - m_f)
        pn = jnp.exp(sn - m_f)
        l_f = al * l_ref[...] + jnp.sum(pn, axis=-1, keepdims=True)
        acc_f = al * acc_ref[...] + jnp.dot(pn.astype(BF16), cn, preferred_element_type=F32)
        _uv_project(acc_f, l_f, wkv_ref, o_ref, DEC_SEQ)


def _sample_attn(page_table_flat, q_lat, q_pe, ckv, kpe, w_ukv_l, cache_ckv, cache_kpe, layer):
    n_pool = cache_ckv.shape[1]
    cc = cache_ckv.reshape(DEPTH * n_pool, PAGE_SIZE, B_KV_RANK)
    ck = jnp.swapaxes(cache_kpe, 2, 3).reshape(DEPTH * n_pool, B_ROPE, PAGE_SIZE)
    rb0 = M_P // DEC_SEQ

    def page_idx(j):
        return lambda b, c, pt: (layer * n_pool + pt[b * N_PAGES + c * SA_PG + j], 0, 0)

    in_specs = [
        pl.BlockSpec((B_HEADS, DEC_SEQ, B_KV_RANK), lambda b, c, pt: (0, rb0 + b, 0)),
        pl.BlockSpec((B_HEADS, DEC_SEQ, B_ROPE), lambda b, c, pt: (0, rb0 + b, 0)),
        pl.BlockSpec((DEC_SEQ, B_KV_RANK), lambda b, c, pt: (rb0 + b, 0)),
        pl.BlockSpec((DEC_SEQ, B_ROPE), lambda b, c, pt: (rb0 + b, 0)),
        pl.BlockSpec((B_KV_RANK, B_HEADS * (B_NOPE + B_V)), lambda b, c, pt: (0, 0)),
    ]
    in_specs += [pl.BlockSpec((None, PAGE_SIZE, B_KV_RANK), page_idx(j)) for j in range(SA_PG)]
    in_specs += [pl.BlockSpec((None, B_ROPE, PAGE_SIZE), page_idx(j)) for j in range(SA_PG)]
    grid_spec = pltpu.PrefetchScalarGridSpec(
        num_scalar_prefetch=1,
        grid=(DEC_BATCH, SA_NCH),
        in_specs=in_specs,
        out_specs=pl.BlockSpec((DEC_SEQ, B_HEADS * B_V), lambda b, c, pt: (b, 0)),
        scratch_shapes=[pltpu.VMEM((SA_ROWS, 1), F32), pltpu.VMEM((SA_ROWS, 1), F32),
                        pltpu.VMEM((SA_ROWS, B_KV_RANK), F32),
                        pltpu.VMEM((SA_PG * PAGE_SIZE, B_KV_RANK), BF16), pltpu.VMEM((B_ROPE, SA_PG * PAGE_SIZE), BF16)],
    )
    return pl.pallas_call(
        _sattn_kernel,
        grid_spec=grid_spec,
        out_shape=jax.ShapeDtypeStruct((M_S, B_HEADS * B_V), F32),
        compiler_params=_cparams("arbitrary", "arbitrary"),
        name="sample_attn",
    )(page_table_flat, q_lat, q_pe, ckv, kpe, w_ukv_l, *([cc] * SA_PG), *([ck] * SA_PG))


def _cmlp_kernel(u_ref, v_ref, nw_ref, wp_ref, ws_ref, bp_ref, bs_ref, o_ref, vn_ref):
    i = pl.program_id(0)
    u = _gelu(u_ref[...])
    v = _gelu(v_ref[...])
    vn = v * lax.rsqrt(jnp.mean(v * v, axis=-1, keepdims=True) + EPS) * nw_ref[...]
    vn_ref[...] = vn
    vb = vn.astype(BF16)

    def run(w_ref, b_ref):
        for g in range(C_GROUPS):
            cols = slice(g * C_GROUP_DIM, (g + 1) * C_GROUP_DIM)
            mixed = jnp.dot(w_ref[g], vb[:, cols], preferred_element_type=F32) + b_ref[g]
            o_ref[:, cols] = u[:, cols] * mixed

    @pl.when(i < NT_P)
    def _():
        run(wp_ref, bp_ref)

    @pl.when(i >= NT_P)
    def _():
        run(ws_ref, bs_ref)


def _block_diag_ws(w_s_l, b_s_l, length):
    reps = TM // length
    r_i = lax.broadcasted_iota(jnp.int32, (TM, TM), 0)
    c_i = lax.broadcasted_iota(jnp.int32, (TM, TM), 1)
    keep = jnp.logical_and(r_i // length == c_i // length, r_i >= c_i)
    wbd = jnp.where(keep[None], jnp.tile(w_s_l[:, :length, :length], (1, reps, reps)), 0.0).astype(BF16)
    bcol = jnp.tile(b_s_l[:, :length], (1, reps)).reshape(C_GROUPS, TM, 1)
    return wbd, bcol


def _chunk_mlp(z, c_vnorm_l, w_s_l, b_s_l):
    wp, bp = _block_diag_ws(w_s_l, b_s_l, C_CHUNK)
    ws, bs = _block_diag_ws(w_s_l, b_s_l, DEC_SEQ)
    wspec = pl.BlockSpec((C_GROUPS, TM, TM), lambda i: (0, 0, 0))
    bspec = pl.BlockSpec((C_GROUPS, TM, 1), lambda i: (0, 0, 0))
    row = pl.BlockSpec((TM, C_WIDTH), lambda i: (i, 0))
    return pl.pallas_call(
        _cmlp_kernel,
        grid=(NT,),
        in_specs=[
            pl.BlockSpec((TM, C_WIDTH), lambda i: (i, Z_U // C_WIDTH)),
            pl.BlockSpec((TM, C_WIDTH), lambda i: (i, Z_V // C_WIDTH)),
            pl.BlockSpec((1, C_WIDTH), lambda i: (0, 0)),
            wspec, wspec, bspec, bspec,
        ],
        out_specs=[row, row],
        out_shape=[jax.ShapeDtypeStruct((M_ALL, C_WIDTH), F32), jax.ShapeDtypeStruct((M_ALL, C_WIDTH), F32)],
        compiler_params=_cparams("arbitrary"),
        name="chunk_mlp",
    )(z, z, c_vnorm_l.reshape(1, -1), wp, ws, bp, bs)


def _prep_w_in_tail(w):
    o = Z1_WIDTH
    segs = {}
    for name, width in (("a", A_HEADS), ("bt", A_HEADS), ("cq", B_Q_RANK), ("ckv", B_KV_RANK), ("kpe", B_ROPE),
                        ("u", C_WIDTH), ("v", C_WIDTH)):
        segs[name] = w[:, o:o + width]
        o += width
    assert o == w.shape[1]
    zeros = lambda n: jnp.zeros((w.shape[0], n), w.dtype)
    parts = [segs["u"], segs["v"], segs["ckv"], segs["cq"],
             segs["kpe"], zeros(LANES - B_ROPE), segs["a"], segs["bt"], zeros(LANES - 2 * A_HEADS)]
    out = jnp.concatenate(parts, axis=1).astype(BF16)
    assert out.shape[1] == Z_WIDTH
    return out


def _prep_w_uq(w):
    w3 = w.reshape(B_Q_RANK, B_HEADS, B_NOPE + B_ROPE)
    return jnp.concatenate([w3[:, :, :B_NOPE].reshape(B_Q_RANK, NOPE_W), w3[:, :, B_NOPE:].reshape(B_Q_RANK, ROPE_W)],
                           axis=1)


def _rope_tables():
    half = B_ROPE // 2
    inv = 1.0 / (ROPE_THETA ** (jnp.arange(half, dtype=F32) / half))
    pos = jnp.concatenate([jnp.tile(jnp.arange(SEQ, dtype=jnp.int32), BATCH),
                           jnp.tile(PAST_LEN + jnp.arange(DEC_SEQ, dtype=jnp.int32), DEC_BATCH)])
    ang = pos.astype(F32)[:, None] * inv[None, :]
    cos, sin = jnp.cos(ang), jnp.sin(ang)
    cos_t = jnp.tile(jnp.concatenate([cos, cos], axis=1), (1, B_HEADS))
    sin_t = jnp.tile(jnp.concatenate([-sin, sin], axis=1), (1, B_HEADS))
    return cos_t, sin_t


def _lane_row(vec):
    return jnp.zeros((1, LANES), F32).at[0, :vec.shape[0]].set(vec.astype(F32))


def kernel(x_prompt, x_sample, cache_ckv, cache_kpe, state_gdn, state_conv, page_table, c_prompt, c_sample, w_ada, b_ada, w_in, conv_w, a_log, dt_bias, gdn_norm, q_norm, kv_norm, w_uq, w_ukv, c_vnorm, w_s, b_s, w_out, w_gate, w_up, w_down, w_router, e_gate, e_up, e_down, final_norm):
    x = jnp.concatenate([x_prompt.reshape(M_P, D_MODEL), x_sample.reshape(M_S, D_MODEL)], axis=0)
    c_all = jnp.concatenate([c_prompt, c_sample, jnp.zeros((N_COND_PAD - N_COND, D_MODEL), F32)], axis=0)
    mod = _ada(c_all, w_ada, b_ada)
    cos_t, sin_t = _rope_tables()
    pt_flat = page_table.reshape(-1)

    def mods(l):
        return (mod[l, :BATCH].reshape(BATCH, 6, 1, D_MODEL), mod[l, BATCH:N_COND].reshape(DEC_BATCH, 6, 1, D_MODEL))

    outs = {k: [] for k in ("p_ckv", "p_kpe", "p_gdn", "p_conv", "s_ckv", "s_kpe", "s_gdn", "s_conv", "s_cv")}
    mp, ms = mods(0)
    h = _normmod(x, mp, ms, 0, 1)
    y = None
    for l in range(DEPTH):
        z1 = _mm(h, w_in, l, tn=512, name="in_proj_a", n_cols=Z1_WIDTH)
        z2 = _mm(h, _prep_w_in_tail(w_in[l]), 0, tn=1024, name="in_proj_b")
        st_pad = jnp.concatenate([jnp.zeros((DEC_BATCH, SUBLANES - (A_CONV - 1), A_CONV_CH), F32), state_conv[l]],
                                 axis=1).reshape(M_S, A_CONV_CH)
        qkv = _conv_qkv(z1, st_pad, conv_w[l])
        gate_rows = (_lane_row(a_log[l]), _lane_row(dt_bias[l]), gdn_norm[l].reshape(1, A_DV).astype(F32))
        oa_p, sg_p = _gdn(qkv, z1, z2, gate_rows, None, row0=0, nseq=BATCH, seq_len=SEQ, c=A_CHUNK, hb=GDN_HB_PROMPT,
                          name="gdn_prompt")
        oa_s, sg_s = _gdn(qkv, z1, z2, gate_rows, state_gdn.reshape((-1,) + state_gdn.shape[2:]), row0=M_P,
                          nseq=DEC_BATCH, seq_len=DEC_SEQ, c=DEC_SEQ, hb=A_HEADS, name="gdn_sample",
                          s0_seq0=l * DEC_BATCH)
        q_lat, q_pe, ckv, kpe = _mla_proj(z2, cos_t, sin_t, q_norm[l], kv_norm[l], _prep_w_uq(w_uq[l]), w_ukv[l])
        ob_p = _prompt_attn(q_lat, q_pe, ckv, kpe, w_ukv[l])
        ob_s = _sample_attn(pt_flat, q_lat, q_pe, ckv, kpe, w_ukv[l], cache_ckv, cache_kpe, l)
        o_c, vn = _chunk_mlp(z2, c_vnorm[l], w_s[l], b_s[l])
        cat = jnp.concatenate([jnp.concatenate([oa_p, oa_s], axis=0), jnp.concatenate([ob_p, ob_s], axis=0), o_c],
                              axis=1).astype(BF16)
        f = _mm(cat, w_out, l, tn=512, name="out_proj")
        x, h = _resid(x, [f], None, mp, ms, 2, nxt=(mp, ms, 3, 4))

        xq = z1[:, :A_CONV_CH]
        outs["p_ckv"].append(ckv[:M_P].reshape(BATCH, SEQ, B_KV_RANK))
        outs["p_kpe"].append(kpe[:M_P].reshape(BATCH, SEQ, B_ROPE))
        outs["p_gdn"].append(sg_p)
        outs["p_conv"].append(xq[:M_P].reshape(BATCH, SEQ, A_CONV_CH)[:, SEQ - (A_CONV - 1):])
        outs["s_ckv"].append(ckv[M_P:].reshape(DEC_BATCH, DEC_SEQ, B_KV_RANK))
        outs["s_kpe"].append(kpe[M_P:].reshape(DEC_BATCH, DEC_SEQ, B_ROPE))
        outs["s_gdn"].append(sg_s)
        outs["s_conv"].append(xq[M_P:].reshape(DEC_BATCH, DEC_SEQ, A_CONV_CH)[:, DEC_SEQ - (A_CONV - 1):])
        outs["s_cv"].append(vn[M_P:].reshape(DEC_BATCH, DEC_SEQ, C_WIDTH))

        m_idx = l // 2
        if l % 2 == 0:
            hid = _swiglu_up(h, w_gate.reshape((-1, 1) + w_gate.shape[1:]), w_up.reshape((-1, 1) + w_up.shape[1:]),
                             m_idx, _dense_plan(M_ALL // UP_TM))
            fs = [_swiglu_down(hid, w_down.reshape((-1, 1) + w_down.shape[1:]), m_idx, _dense_plan(M_ALL // DOWN_TM))]
            tw = None
        else:
            w_r = jnp.zeros((D_MODEL, LANES), BF16).at[:, :N_EXPERTS].set(w_router[m_idx].astype(BF16))
            ti, tw = _router(h, w_r)
            src, pos, plan_up, plan_down = _route_plan(ti[:, :TOP_K])
            xs = jnp.take(h, src, axis=0, mode="clip")
            hid = _swiglu_up(xs, e_gate, e_up, m_idx, plan_up)
            ys = _swiglu_down(hid, e_down, m_idx, plan_down)
            fs = [jnp.take(ys, pos[:, k], axis=0, mode="clip") for k in range(TOP_K)]
        if l + 1 < DEPTH:
            mp_n, ms_n = mods(l + 1)
            x, h = _resid(x, fs, tw, mp, ms, 5, nxt=(mp_n, ms_n, 0, 1))
            mp, ms = mp_n, ms_n
        else:
            y_p, y_s = _resid(x, fs, tw, mp, ms, 5, final_w=final_norm)

    st = lambda k: jnp.stack(outs[k])
    return (y_p.reshape(BATCH, SEQ, D_MODEL), y_s.reshape(DEC_BATCH, DEC_SEQ, D_MODEL),
            st("p_ckv"), st("p_kpe"), st("p_gdn"), st("p_conv"),
            st("s_ckv"), st("s_kpe"), st("s_gdn"), st("s_conv"), st("s_cv"))
```

```python
import functools

import numpy as np
import jax
import jax.numpy as jnp
from jax import lax
from jax.experimental import pallas as pl
from jax.experimental.pallas import tpu as pltpu

D_MODEL = 4096
BATCH = 4
SEQ = 2048
DEPTH = 2
DEC_BATCH = 128
DEC_SEQ = 8
PAST_LEN = 16384
PAGE_SIZE = 128
A_HEADS = 12
A_DK = 128
A_DV = 128
A_CONV = 4
A_CHUNK = 64
B_HEADS = 12
B_Q_RANK = 768
B_KV_RANK = 256
B_NOPE = 128
B_ROPE = 64
B_V = 128
ROPE_THETA = 10000.0
C_GROUPS = 8
C_GROUP_DIM = 128
C_CHUNK = 128
D_FF = 14336
N_EXPERTS = 8
TOP_K = 2
EPS = 1e-6

A_QK_WIDTH = A_HEADS * A_DK
A_V_WIDTH = A_HEADS * A_DV
A_CONV_CH = 2 * A_QK_WIDTH + A_V_WIDTH
C_WIDTH = C_GROUPS * C_GROUP_DIM
MIX_WIDTH = A_V_WIDTH + B_HEADS * B_V + C_WIDTH
B_SCALE = (B_NOPE + B_ROPE) ** -0.5
N_PAGES = PAST_LEN // PAGE_SIZE

F32 = jnp.float32
BF16 = jnp.bfloat16
LANES = 128
SUBLANES = 8
VMEM_LIMIT = 56 * 1024 * 1024

M_P = BATCH * SEQ
M_S = DEC_BATCH * DEC_SEQ
M_ALL = M_P + M_S
N_COND = BATCH + DEC_BATCH
N_COND_PAD = -(-N_COND // SUBLANES) * SUBLANES

TM = 512
NT_P = M_P // TM
NT = M_ALL // TM
TE = 256
NE_P = M_P // TE
NE = M_ALL // TE
SEQ_TE = SEQ // TE
SB_TE = TE // DEC_SEQ

Z_XQKV = 0
Z_GATE = A_CONV_CH
Z1_WIDTH = A_CONV_CH + A_V_WIDTH
Z_U = 0
Z_V = 1024
Z_CKV = 2048
Z_CQ = 2304
Z_KPE = 3072
Z_AB = 3200
Z_WIDTH = 3328


def _cparams(*sem):
    return pltpu.CompilerParams(dimension_semantics=sem, vmem_limit_bytes=VMEM_LIMIT)


def _silu(x):
    return x * jax.nn.sigmoid(x)


def _gelu(x):
    return 0.5 * x * (1.0 + lax.erf(x * (2.0 ** -0.5)))


def _cast_rows(dst_ref, src_ref, chunk=256):
    rows = src_ref.shape[0]
    chunk = min(chunk, rows)

    def body(i, c):
        r = pl.multiple_of(i * chunk, chunk)
        dst_ref[pl.ds(r, chunk), :] = src_ref[pl.ds(r, chunk), :].astype(dst_ref.dtype)
        return c

    lax.fori_loop(0, rows // chunk, body, 0)


def _ada_kernel(c_ref, w_ref, b_ref, o_ref):
    a = _silu(c_ref[...]).astype(BF16)
    o_ref[...] = jnp.dot(a, w_ref[...].astype(BF16), preferred_element_type=F32) + b_ref[...]


def _ada(c_all, w_ada, b_ada):
    nl = w_ada.shape[0]
    tn = 512
    n6 = 6 * D_MODEL
    return pl.pallas_call(
        _ada_kernel,
        grid=(nl, n6 // tn),
        in_specs=[
            pl.BlockSpec((N_COND_PAD, D_MODEL), lambda l, n: (0, 0)),
            pl.BlockSpec((None, D_MODEL, tn), lambda l, n: (l, 0, n)),
            pl.BlockSpec((None, 1, tn), lambda l, n: (l, 0, n)),
        ],
        out_specs=pl.BlockSpec((None, N_COND_PAD, tn), lambda l, n: (l, 0, n)),
        out_shape=jax.ShapeDtypeStruct((nl, N_COND_PAD, n6), F32),
        compiler_params=_cparams("arbitrary", "arbitrary"),
        name="ada",
    )(c_all, w_ada, b_ada.reshape(nl, 1, n6))


def _mod_specs(which):
    sp = pl.BlockSpec((1, None, 1, D_MODEL), lambda i: (jnp.minimum(i // SEQ_TE, BATCH - 1), which, 0, 0))
    ss = pl.BlockSpec((SB_TE, None, 1, D_MODEL), lambda i: (jnp.maximum(i - NE_P, 0), which, 0, 0))
    return sp, ss


def _per_group(i, fn):
    @pl.when(i < NE_P)
    def _():
        fn(0)

    @pl.when(i >= NE_P)
    def _():
        fn(1)


def _rms(x3):
    return x3 * lax.rsqrt(jnp.mean(x3 * x3, axis=-1, keepdims=True) + EPS)


def _normmod_kernel(x_ref, shp_ref, shs_ref, scp_ref, scs_ref, h_ref):
    i = pl.program_id(0)

    def run(g):
        ns = (1, SB_TE)[g]
        sh = (shp_ref, shs_ref)[g][...]
        sc = (scp_ref, scs_ref)[g][...]
        x3 = x_ref[...].reshape(ns, TE // ns, D_MODEL)
        h = _rms(x3) * (1.0 + sc) + sh
        h_ref[...] = h.reshape(TE, D_MODEL).astype(h_ref.dtype)

    _per_group(i, run)


def _normmod(x, mod_p, mod_s, i_shift, i_scale):
    shp, shs = _mod_specs(i_shift)
    scp, scs = _mod_specs(i_scale)
    row = pl.BlockSpec((TE, D_MODEL), lambda i: (i, 0))
    return pl.pallas_call(
        _normmod_kernel,
        grid=(NE,),
        in_specs=[row, shp, shs, scp, scs],
        out_specs=row,
        out_shape=jax.ShapeDtypeStruct((M_ALL, D_MODEL), BF16),
        compiler_params=_cparams("arbitrary"),
        name="normmod",
    )(x, mod_p, mod_s, mod_p, mod_s)


def _resid_kernel(*refs, n_f, mode):
    i = pl.program_id(0)
    x_ref, f0_ref = refs[0], refs[1]
    k = 2
    if n_f == 2:
        f1_ref, tw_ref = refs[2], refs[3]
        k = 4
    gp_ref, gs_ref = refs[k], refs[k + 1]
    if mode == "mod":
        shp_ref, shs_ref, scp_ref, scs_ref, y_ref, h_ref = refs[k + 2:k + 8]
    else:
        fw_ref, outp_ref, outs_ref = refs[k + 2:k + 5]

    def run(g):
        ns = (1, SB_TE)[g]
        f = f0_ref[...]
        if n_f == 2:
            tw = tw_ref[...]
            f = f * tw[:, 0:1] + f1_ref[...] * tw[:, 1:2]
        shape3 = (ns, TE // ns, D_MODEL)
        gate = (gp_ref, gs_ref)[g][...]
        y3 = x_ref[...].reshape(shape3) + gate * f.reshape(shape3)
        if mode == "mod":
            sh = (shp_ref, shs_ref)[g][...]
            sc = (scp_ref, scs_ref)[g][...]
            y_ref[...] = y3.reshape(TE, D_MODEL)
            h_ref[...] = (_rms(y3) * (1.0 + sc) + sh).reshape(TE, D_MODEL).astype(h_ref.dtype)
        else:
            (outp_ref, outs_ref)[g][...] = (_rms(y3) * fw_ref[...]).reshape(TE, D_MODEL)

    _per_group(i, run)


def _resid(x, fs, tw, mod_p, mod_s, i_gate, nxt=None, final_w=None):
    row = pl.BlockSpec((TE, D_MODEL), lambda i: (i, 0))
    n_f = len(fs)
    args = [x] + list(fs)
    specs = [row] * (1 + n_f)
    if n_f == 2:
        args.append(tw)
        specs.append(pl.BlockSpec((TE, LANES), lambda i: (i, 0)))
    gp, gs = _mod_specs(i_gate)
    args += [mod_p, mod_s]
    specs += [gp, gs]
    if nxt is not None:
        np_, ns_, i_shift, i_scale = nxt
        shp, shs = _mod_specs(i_shift)
        scp, scs = _mod_specs(i_scale)
        args += [np_, ns_, np_, ns_]
        specs += [shp, shs, scp, scs]
        out_specs = [row, row]
        out_shape = [jax.ShapeDtypeStruct((M_ALL, D_MODEL), F32), jax.ShapeDtypeStruct((M_ALL, D_MODEL), BF16)]
        mode = "mod"
    else:
        args.append(final_w.reshape(1, 1, D_MODEL))
        specs.append(pl.BlockSpec((1, 1, D_MODEL), lambda i: (0, 0, 0)))
        out_specs = [pl.BlockSpec((TE, D_MODEL), lambda i: (jnp.minimum(i, NE_P - 1), 0)),
                     pl.BlockSpec((TE, D_MODEL), lambda i: (jnp.maximum(i - NE_P, 0), 0))]
        out_shape = [jax.ShapeDtypeStruct((M_P, D_MODEL), F32), jax.ShapeDtypeStruct((M_S, D_MODEL), F32)]
        mode = "final"
    return pl.pallas_call(
        functools.partial(_resid_kernel, n_f=n_f, mode=mode),
        grid=(NE,),
        in_specs=specs,
        out_specs=out_specs,
        out_shape=out_shape,
        compiler_params=_cparams("arbitrary"),
        name="resid_" + mode,
    )(*args)


def _mm_kernel(x_ref, w_ref, o_ref, *scratch, cast_w, w_t):
    if cast_w:
        wb_ref, = scratch

        @pl.when(pl.program_id(1) == 0)
        def _():
            _cast_rows(wb_ref, w_ref)

        w = wb_ref[...]
    else:
        w = w_ref[...]
    dn = _DN_NT if w_t else _DN_NN
    o_ref[...] = lax.dot_general(x_ref[...], w, dn, preferred_element_type=F32).astype(o_ref.dtype)


def _mm(x, w, layer, tn, out_dtype=F32, name="mm", n_cols=None, w_t=False):
    m, k = x.shape
    n = w.shape[1 if w_t else -1] if n_cols is None else n_cols
    cast_w = w.dtype != BF16
    if w_t:
        w_spec = pl.BlockSpec((None, tn, k), lambda j, i: (layer, j, 0))
    elif w.ndim == 3:
        w_spec = pl.BlockSpec((None, k, tn), lambda j, i: (layer, 0, j))
    else:
        w_spec = pl.BlockSpec((k, tn), lambda j, i: (0, j))
    scratch = [pltpu.VMEM((tn, k) if w_t else (k, tn), BF16)] if cast_w else []
    return pl.pallas_call(
        functools.partial(_mm_kernel, cast_w=cast_w, w_t=w_t),
        grid=(pl.cdiv(n, tn), m // TM),
        in_specs=[pl.BlockSpec((TM, k), lambda j, i: (i, 0)), w_spec],
        out_specs=pl.BlockSpec((TM, tn), lambda j, i: (i, j)),
        out_shape=jax.ShapeDtypeStruct((m, n), out_dtype),
        scratch_shapes=scratch,
        compiler_params=_cparams("arbitrary", "arbitrary"),
        name=name,
    )(x, w)


UP_TM = 512
UP_SUB = 256
DOWN_TM = 1024
DOWN_SUB = 512
assert M_ALL % DOWN_TM == 0


def _swiglu_kernel(plan_ref, x_ref, wg_ref, wu_ref, o_ref, wgb_ref, wub_ref):
    i = pl.program_id(1)
    e_prev = plan_ref[0, jnp.maximum(i - 1, 0)]
    changed = jnp.logical_or(i == 0, plan_ref[0, i] != e_prev)
    nsub = plan_ref[1, i]

    @pl.when(changed)
    def _():
        _cast_rows(wgb_ref, wg_ref)
        _cast_rows(wub_ref, wu_ref)

    def compute(r0):
        x = x_ref[r0:, :]
        g = jnp.dot(x, wgb_ref[...], preferred_element_type=F32)
        u = jnp.dot(x, wub_ref[...], preferred_element_type=F32)
        o_ref[r0:, :] = (_silu(g) * u).astype(o_ref.dtype)
        if r0:
            o_ref[0:r0, :] = jnp.zeros((r0, o_ref.shape[1]), o_ref.dtype)

    @pl.when(nsub == 2)
    def _():
        compute(0)

    @pl.when(nsub == 1)
    def _():
        compute(UP_TM - UP_SUB)

    @pl.when(nsub == 0)
    def _():
        o_ref[...] = jnp.zeros_like(o_ref)


def _swiglu_up(x, w_gate, w_up, layer, plan, tn=512):
    m = x.shape[0]
    w_spec = pl.BlockSpec((None, None, D_MODEL, tn), lambda j, i, pr: (layer, pr[0, i], 0, j))
    grid_spec = pltpu.PrefetchScalarGridSpec(
        num_scalar_prefetch=1,
        grid=(D_FF // tn, m // UP_TM),
        in_specs=[pl.BlockSpec((UP_TM, D_MODEL), lambda j, i, pr: (pr[2, i], 0)), w_spec, w_spec],
        out_specs=pl.BlockSpec((UP_TM, tn), lambda j, i, pr: (i, j)),
        scratch_shapes=[pltpu.VMEM((D_MODEL, tn), BF16), pltpu.VMEM((D_MODEL, tn), BF16)],
    )
    return pl.pallas_call(
        _swiglu_kernel,
        grid_spec=grid_spec,
        out_shape=jax.ShapeDtypeStruct((m, D_FF), BF16),
        compiler_params=_cparams("arbitrary", "arbitrary"),
        name="swiglu_up",
    )(plan, x, w_gate, w_up)


def _down_kernel(plan_ref, x_ref, w_ref, o_ref):
    i = pl.program_id(0)
    kk = pl.program_id(2)
    nsub = plan_ref[1, i]

    def accumulate(r0):
        p = jnp.dot(x_ref[r0:, :], w_ref[...].astype(BF16), preferred_element_type=F32)

        @pl.when(kk == 0)
        def _():
            o_ref[r0:, :] = p

        @pl.when(kk > 0)
        def _():
            o_ref[r0:, :] += p

    @pl.when(nsub == 2)
    def _():
        accumulate(0)

    @pl.when(nsub == 1)
    def _():
        accumulate(DOWN_TM - DOWN_SUB)

    @pl.when(jnp.logical_and(kk == 0, nsub < 2))
    def _():
        o_ref[0:DOWN_TM - DOWN_SUB, :] = jnp.zeros((DOWN_TM - DOWN_SUB, o_ref.shape[1]), o_ref.dtype)

    @pl.when(jnp.logical_and(kk == 0, nsub < 1))
    def _():
        o_ref[DOWN_TM - DOWN_SUB:, :] = jnp.zeros((DOWN_SUB, o_ref.shape[1]), o_ref.dtype)


def _swiglu_down(hid, w_down, layer, plan, tn=1024, tk=2048):
    m = hid.shape[0]
    grid_spec = pltpu.PrefetchScalarGridSpec(
        num_scalar_prefetch=1,
        grid=(m // DOWN_TM, D_MODEL // tn, D_FF // tk),
        in_specs=[
            pl.BlockSpec((DOWN_TM, tk), lambda i, j, k, pr: (pr[2, i], k)),
            pl.BlockSpec((None, None, tk, tn), lambda i, j, k, pr: (layer, pr[0, i], k, j)),
        ],
        out_specs=pl.BlockSpec((DOWN_TM, tn), lambda i, j, k, pr: (i, j)),
    )
    return pl.pallas_call(
        _down_kernel,
        grid_spec=grid_spec,
        out_shape=jax.ShapeDtypeStruct((m, D_MODEL), F32),
        compiler_params=_cparams("arbitrary", "arbitrary", "arbitrary"),
        name="swiglu_down",
    )(plan, hid, w_down)


def _dense_plan(tiles):
    t = jnp.arange(tiles, dtype=jnp.int32)
    return jnp.stack([jnp.zeros_like(t), jnp.full_like(t, 2), t])


def _router_kernel(h_ref, w_ref, ti_ref, tw_ref):
    logits = jnp.dot(h_ref[...], w_ref[...], preferred_element_type=F32)
    lane = lax.broadcasted_iota(jnp.int32, logits.shape, 1)
    neg = jnp.float32(-jnp.inf)
    logits = jnp.where(lane < N_EXPERTS, logits, neg)
    m1 = jnp.max(logits, axis=-1, keepdims=True)
    i1 = jnp.min(jnp.where(logits == m1, lane, LANES), axis=-1, keepdims=True)
    rest = jnp.where(lane == i1, neg, logits)
    m2 = jnp.max(rest, axis=-1, keepdims=True)
    i2 = jnp.min(jnp.where(rest == m2, lane, LANES), axis=-1, keepdims=True)
    e2 = jnp.exp(m2 - m1)
    den = 1.0 + e2
    ti_ref[...] = jnp.where(lane == 0, i1, jnp.where(lane == 1, i2, 0))
    tw_ref[...] = jnp.where(lane == 0, 1.0 / den, jnp.where(lane == 1, e2 / den, 0.0))


def _router(h, w_router_pad):
    row = pl.BlockSpec((TM, LANES), lambda i: (i, 0))
    return pl.pallas_call(
        _router_kernel,
        grid=(NT,),
        in_specs=[pl.BlockSpec((TM, D_MODEL), lambda i: (i, 0)), pl.BlockSpec((D_MODEL, LANES), lambda i: (0, 0))],
        out_specs=[row, row],
        out_shape=[jax.ShapeDtypeStruct((M_ALL, LANES), jnp.int32), jax.ShapeDtypeStruct((M_ALL, LANES), F32)],
        compiler_params=_cparams("arbitrary"),
        name="router",
    )(h, w_router_pad)


MOE_ROWS = M_ALL * TOP_K + N_EXPERTS * DOWN_TM


def _tile_plan(lo, ends, tm, sub):
    tiles = MOE_ROWS // tm
    idx = jnp.arange(tiles, dtype=jnp.int32)
    t0 = idx * tm
    e = jnp.minimum(jnp.sum((t0[:, None] >= ends[None, :]).astype(jnp.int32), axis=1), N_EXPERTS - 1)
    used_rows = jnp.where(t0 < ends[e], jnp.clip(t0 + tm - lo[e], 0, tm), 0)
    nsub = (used_rows + sub - 1) // sub
    fetch = lax.cummax(jnp.where(nsub > 0, idx, 0), axis=0)
    return jnp.stack([e, nsub, fetch]).astype(jnp.int32)


def _route_plan(ti):
    e_flat = ti.reshape(-1)
    onehot = (e_flat[:, None] == jnp.arange(N_EXPERTS, dtype=jnp.int32)[None, :]).astype(jnp.int32)
    counts = jnp.sum(onehot, axis=0)
    rank = jnp.sum((jnp.cumsum(onehot, axis=0) - onehot) * onehot, axis=1)
    padded = ((counts + DOWN_TM - 1) // DOWN_TM) * DOWN_TM
    ends = jnp.cumsum(padded)
    lo = ends - counts
    pos = lo[e_flat] + rank
    src = jnp.zeros((MOE_ROWS,), jnp.int32).at[pos].set(jnp.arange(e_flat.shape[0], dtype=jnp.int32) // TOP_K)
    return (src, pos.reshape(-1, TOP_K), _tile_plan(lo, ends, UP_TM, UP_SUB), _tile_plan(lo, ends, DOWN_TM, DOWN_SUB))


def _conv_kernel(x_ref, hp_ref, hs_ref, w_ref, o_ref, sp_ref, ss_ref):
    i = pl.program_id(0)
    part = pl.program_id(1)
    width = x_ref.shape[1]
    w = w_ref[...]

    def finish(y):
        y = _silu(y)
        qk = part < 2
        post = jnp.where(part == 0, A_DK ** -0.5, 1.0)
        for h in range(width // A_DK):
            seg = y[:, h * A_DK:(h + 1) * A_DK]
            ss = jnp.sum(seg * seg, axis=-1, keepdims=True)
            scale = jnp.where(qk, lax.rsqrt(ss + EPS), 1.0) * post
            o_ref[:, h * A_DK:(h + 1) * A_DK] = seg * scale

    @pl.when(i < NE_P)
    def _():
        first = (i % SEQ_TE) == 0
        sp_ref[0:SUBLANES, :] = jnp.where(first, 0.0, hp_ref[...])
        sp_ref[SUBLANES:, :] = x_ref[...]
        acc = sp_ref[SUBLANES - 3:SUBLANES - 3 + TE, :] * w[0:1, :]
        for j in range(1, A_CONV):
            acc = acc + sp_ref[SUBLANES - 3 + j:SUBLANES - 3 + j + TE, :] * w[j:j + 1, :]
        finish(acc)

    @pl.when(i >= NE_P)
    def _():
        ss_ref[:, 0:SUBLANES, :] = hs_ref[...].reshape(SB_TE, SUBLANES, width)
        ss_ref[:, SUBLANES:, :] = x_ref[...].reshape(SB_TE, DEC_SEQ, width)
        acc = ss_ref[:, SUBLANES - 3:SUBLANES - 3 + DEC_SEQ, :] * w[0:1, :]
        for j in range(1, A_CONV):
            acc = acc + ss_ref[:, SUBLANES - 3 + j:SUBLANES - 3 + j + DEC_SEQ, :] * w[j:j + 1, :]
        finish(acc.reshape(TE, width))


def _conv_qkv(z, conv_state_pad, conv_w_l):
    width = A_QK_WIDTH
    assert DEC_SEQ == SUBLANES
    return pl.pallas_call(
        _conv_kernel,
        grid=(NE, 3),
        in_specs=[
            pl.BlockSpec((TE, width), lambda i, p: (i, p)),
            pl.BlockSpec((SUBLANES, width), lambda i, p: (jnp.maximum(i * (TE // SUBLANES) - 1, 0), p)),
            pl.BlockSpec((TE, width), lambda i, p: (jnp.maximum(i - NE_P, 0), p)),
            pl.BlockSpec((A_CONV, width), lambda i, p: (0, p)),
        ],
        out_specs=pl.BlockSpec((TE, width), lambda i, p: (i, p)),
        out_shape=jax.ShapeDtypeStruct((M_ALL, A_CONV_CH), F32),
        scratch_shapes=[pltpu.VMEM((TE + SUBLANES, width), F32), pltpu.VMEM((SB_TE, 2 * SUBLANES, width), F32)],
        compiler_params=_cparams("arbitrary", "arbitrary"),
        name="conv_qkv",
    )(z, z, conv_state_pad, conv_w_l)


GDN_HB_PROMPT = A_HEADS

_DN_NN = (((1,), (0,)), ((), ()))
_DN_NT = (((1,), (1,)), ((), ()))
_DN_TN = (((0,), (0,)), ((), ()))


def _dotp(a, b, prec, dn=_DN_NN):
    if prec == "f32":
        return lax.dot_general(a, b, dn, preferred_element_type=F32, precision=lax.Precision.HIGHEST)
    a_hi, b_hi = a.astype(BF16), b.astype(BF16)
    out = lax.dot_general(a_hi, b_hi, dn, preferred_element_type=F32)
    if prec == "bf16x3":
        a_lo = (a - a_hi.astype(F32)).astype(BF16)
        b_lo = (b - b_hi.astype(F32)).astype(BF16)
        out = out + (lax.dot_general(a_hi, b_lo, dn, preferred_element_type=F32)
                     + lax.dot_general(a_lo, b_hi, dn, preferred_element_type=F32))
    return out


def _gdn_kernel(*refs, c, hb, has_s0, p_inv, p_bulk):
    if has_s0:
        q_ref, k_ref, v_ref, zg_ref, ab_ref, al_ref, dt_ref, nw_ref, s0_ref, o_ref, sf_ref, s_ref = refs
    else:
        q_ref, k_ref, v_ref, zg_ref, ab_ref, al_ref, dt_ref, nw_ref, o_ref, sf_ref, s_ref = refs
    hblk = pl.program_id(1)
    ch = pl.program_id(2)

    @pl.when(ch == 0)
    def _():
        if has_s0:
            s_ref[...] = s0_ref[...]
        else:
            s_ref[...] = jnp.zeros_like(s_ref)

    ab = ab_ref[...]
    x = ab + dt_ref[...]
    softplus = jnp.maximum(x, 0.0) + jnp.log1p(jnp.exp(-jnp.abs(x)))
    g_all = -jnp.exp(al_ref[...]) * softplus
    beta_all = jax.nn.sigmoid(ab)
    r_i = lax.broadcasted_iota(jnp.int32, (c, c), 0)
    c_i = lax.broadcasted_iota(jnp.int32, (c, c), 1)
    incl = r_i >= c_i
    strict = r_i > c_i
    eye = (r_i == c_i).astype(F32)
    gc_all = _dotp(incl.astype(F32), g_all, "f32")
    sel = (lax.broadcasted_iota(jnp.int32, (2 * SUBLANES, LANES), 0)
           == lax.broadcasted_iota(jnp.int32, (2 * SUBLANES, LANES), 1)).astype(F32)
    gct_all = _dotp(sel, gc_all, "f32", _DN_NT)
    lane = lax.broadcasted_iota(jnp.int32, (c, LANES), 1)
    sub = lax.broadcasted_iota(jnp.int32, (2 * SUBLANES, c), 0)
    nw = nw_ref[...]

    hs = range(hb)
    cols = [slice(j * A_DK, (j + 1) * A_DK) for j in hs]
    heads = [hblk * hb + j for j in hs]
    q = [q_ref[:, cols[j]] for j in hs]
    k = [k_ref[:, cols[j]] for j in hs]
    v = [v_ref[:, cols[j]] for j in hs]
    s_old = [s_ref[j] for j in hs]
    gc = [jnp.sum(jnp.where(lane == heads[j], gc_all, 0.0), axis=1, keepdims=True) for j in hs]
    beta = [jnp.sum(jnp.where(lane == heads[j] + A_HEADS, beta_all, 0.0), axis=1, keepdims=True) for j in hs]
    gc_row = [jnp.sum(jnp.where(sub == heads[j], gct_all, 0.0), axis=0, keepdims=True) for j in hs]
    gc_last = [gc[j][c - 1:c, :] for j in hs]
    decay = [jnp.where(incl, jnp.exp(jnp.where(incl, gc[j] - gc_row[j], 0.0)), 0.0) for j in hs]
    kb = [k[j] * beta[j] for j in hs]
    x_pow = [-jnp.where(strict, _dotp(kb[j], k[j], p_bulk, _DN_NT) * decay[j], 0.0) for j in hs]
    qk = [_dotp(q[j], k[j], p_bulk, _DN_NT) * decay[j] for j in hs]
    t_inv = [eye + x_pow[j] for j in hs]
    for _ in range(max(int(np.ceil(np.log2(c))) - 1, 0)):
        x_pow = [_dotp(x_pow[j], x_pow[j], p_inv) for j in hs]
        t_inv = [t_inv[j] + _dotp(t_inv[j], x_pow[j], p_inv) for j in hs]
    eg = [jnp.exp(gc[j]) for j in hs]
    wu = [_dotp(t_inv[j], jnp.concatenate([kb[j] * eg[j], v[j] * beta[j]], axis=1), p_bulk) for j in hs]
    qs = [_dotp(q[j] * eg[j], s_old[j], p_bulk) for j in hs]
    v_new = [wu[j][:, A_DK:] - _dotp(wu[j][:, :A_DK], s_old[j], p_bulk) for j in hs]
    o = [qs[j] + _dotp(qk[j], v_new[j], p_bulk) for j in hs]
    kv = [_dotp(k[j] * jnp.exp(gc_last[j] - gc[j]), v_new[j], p_bulk, _DN_TN) for j in hs]
    for j in hs:
        s_ref[j] = s_old[j] * jnp.exp(gc_last[j]) + kv[j]
        o_n = o[j] * lax.rsqrt(jnp.mean(o[j] * o[j], axis=-1, keepdims=True) + EPS) * nw
        o_ref[:, cols[j]] = o_n * _silu(zg_ref[:, cols[j]])

    @pl.when(ch == pl.num_programs(2) - 1)
    def _():
        sf_ref[...] = s_ref[...]


def _gdn(qkv, z1, z2, gate_rows, s0, *, row0, nseq, seq_len, c, hb, name, s0_seq0=0, p_inv="bf16x3", p_bulk="bf16"):
    al_row, dt_row, nw_row = gate_rows
    nch = seq_len // c
    wblk = hb * A_DK
    rb0 = row0 // c
    nq = A_QK_WIDTH // wblk

    def rows(b, h, n):
        return rb0 + b * nch + n

    in_specs = [
        pl.BlockSpec((c, wblk), lambda b, h, n: (rows(b, h, n), h)),
        pl.BlockSpec((c, wblk), lambda b, h, n: (rows(b, h, n), nq + h)),
        pl.BlockSpec((c, wblk), lambda b, h, n: (rows(b, h, n), 2 * nq + h)),
        pl.BlockSpec((c, wblk), lambda b, h, n: (rows(b, h, n), Z_GATE // wblk + h)),
        pl.BlockSpec((c, LANES), lambda b, h, n: (rows(b, h, n), Z_AB // LANES)),
        pl.BlockSpec((1, LANES), lambda b, h, n: (0, 0)),
        pl.BlockSpec((1, LANES), lambda b, h, n: (0, 0)),
        pl.BlockSpec((1, LANES), lambda b, h, n: (0, 0)),
    ]
    args = [qkv, qkv, qkv, z1, z2, al_row, dt_row, nw_row]
    st_spec = pl.BlockSpec((None, hb, A_DK, A_DV), lambda b, h, n: (b, h, 0, 0))
    if s0 is not None:
        in_specs.append(pl.BlockSpec((None, hb, A_DK, A_DV), lambda b, h, n: (s0_seq0 + b, h, 0, 0)))
        args.append(s0)
    return pl.pallas_call(
        functools.partial(_gdn_kernel, c=c, hb=hb, has_s0=s0 is not None, p_inv=p_inv, p_bulk=p_bulk),
        grid=(nseq, A_HEADS // hb, nch),
        in_specs=in_specs,
        out_specs=[pl.BlockSpec((c, wblk), lambda b, h, n: (b * nch + n, h)), st_spec],
        out_shape=[jax.ShapeDtypeStruct((nseq * seq_len, A_V_WIDTH), F32),
                   jax.ShapeDtypeStruct((nseq, A_HEADS, A_DK, A_DV), F32)],
        scratch_shapes=[pltpu.VMEM((hb, A_DK, A_DV), F32)],
        compiler_params=_cparams("arbitrary", "arbitrary", "arbitrary"),
        name=name,
    )(*args)


ROPE_W = B_HEADS * B_ROPE
NOPE_W = B_HEADS * B_NOPE


def _rope_lanes(x, cos, sin_signed):
    w = x.shape[-1]
    half = B_ROPE // 2
    lane = lax.broadcasted_iota(jnp.int32, x.shape, 1)
    partner = jnp.where((lane % B_ROPE) < half, pltpu.roll(x, w - half, 1), pltpu.roll(x, half, 1))
    return x * cos + partner * sin_signed


def _mla_proj_kernel(cq_ref, ckv_ref, kpe_ref, cos_ref, sin_ref, qn_ref, kvn_ref, wq_ref, wkv_ref,
                     ql_ref, qp_ref, ckvo_ref, kpeo_ref, wqb_ref, wkvb_ref):
    @pl.when(pl.program_id(0) == 0)
    def _():
        _cast_rows(wqb_ref, wq_ref)
        _cast_rows(wkvb_ref, wkv_ref)

    cq = cq_ref[...]
    cqn = cq * lax.rsqrt(jnp.mean(cq * cq, axis=-1, keepdims=True) + EPS) * qn_ref[...]
    q = jnp.dot(cqn.astype(BF16), wqb_ref[...], preferred_element_type=F32)
    cos = cos_ref[...]
    sin = sin_ref[...]
    q_pe = _rope_lanes(q[:, NOPE_W:], cos, sin)
    for h in range(B_HEADS):
        q_nope = q[:, h * B_NOPE:(h + 1) * B_NOPE].astype(BF16)
        w_uk = wkvb_ref[:, h * (B_NOPE + B_V):h * (B_NOPE + B_V) + B_NOPE]
        ql_ref[h] = lax.dot_general(q_nope, w_uk, (((1,), (1,)), ((), ())),
                                    preferred_element_type=F32).astype(ql_ref.dtype)
        qp_ref[h] = q_pe[:, h * B_ROPE:(h + 1) * B_ROPE].astype(qp_ref.dtype)
    ckv = ckv_ref[...]
    ckvo_ref[...] = ckv * lax.rsqrt(jnp.mean(ckv * ckv, axis=-1, keepdims=True) + EPS) * kvn_ref[...]
    kpe = _rope_lanes(kpe_ref[...], cos[:, :LANES], sin[:, :LANES])
    kpeo_ref[...] = kpe[:, :B_ROPE]


def _mla_proj(z, cos_t, sin_t, q_norm_l, kv_norm_l, w_uq_perm, w_ukv_l):
    full = lambda shape: pl.BlockSpec(shape, lambda i: (0,) * len(shape))
    return pl.pallas_call(
        _mla_proj_kernel,
        grid=(NT,),
        in_specs=[
            pl.BlockSpec((TM, B_Q_RANK), lambda i: (i, Z_CQ // B_Q_RANK)),
            pl.BlockSpec((TM, B_KV_RANK), lambda i: (i, Z_CKV // B_KV_RANK)),
            pl.BlockSpec((TM, LANES), lambda i: (i, Z_KPE // LANES)),
            pl.BlockSpec((TM, ROPE_W), lambda i: (i, 0)),
            pl.BlockSpec((TM, ROPE_W), lambda i: (i, 0)),
            full((1, B_Q_RANK)), full((1, B_KV_RANK)),
            full((B_Q_RANK, NOPE_W + ROPE_W)), full((B_KV_RANK, B_HEADS * (B_NOPE + B_V))),
        ],
        out_specs=[
            pl.BlockSpec((B_HEADS, TM, B_KV_RANK), lambda i: (0, i, 0)),
            pl.BlockSpec((B_HEADS, TM, B_ROPE), lambda i: (0, i, 0)),
            pl.BlockSpec((TM, B_KV_RANK), lambda i: (i, 0)),
            pl.BlockSpec((TM, B_ROPE), lambda i: (i, 0)),
        ],
        out_shape=[
            jax.ShapeDtypeStruct((B_HEADS, M_ALL, B_KV_RANK), F32),
            jax.ShapeDtypeStruct((B_HEADS, M_ALL, B_ROPE), F32),
            jax.ShapeDtypeStruct((M_ALL, B_KV_RANK), F32),
            jax.ShapeDtypeStruct((M_ALL, B_ROPE), F32),
        ],
        scratch_shapes=[pltpu.VMEM((B_Q_RANK, NOPE_W + ROPE_W), BF16),
                        pltpu.VMEM((B_KV_RANK, B_HEADS * (B_NOPE + B_V)), BF16)],
        compiler_params=_cparams("arbitrary"),
        name="mla_proj",
    )(z, z, z, cos_t, sin_t, q_norm_l.reshape(1, -1), kv_norm_l.reshape(1, -1), w_uq_perm, w_ukv_l)


def _uv_project(acc, l, wkv_ref, o_ref, rows):
    o_lat = acc / l
    for h in range(B_HEADS):
        lo = h * (B_NOPE + B_V) + B_NOPE
        w_uv = wkv_ref[:, lo:lo + B_V].astype(BF16)
        o_h = o_lat[h * rows:(h + 1) * rows, :].astype(BF16)
        o_ref[:, h * B_V:(h + 1) * B_V] = jnp.dot(o_h, w_uv, preferred_element_type=F32)


PA_TQ = 128
PA_TK = 512


def _pattn_kernel(ql_ref, qp_ref, ckv_ref, kpe_ref, wkv_ref, o_ref, m_ref, l_ref, acc_ref):
    qb = pl.program_id(1)
    kb = pl.program_id(2)
    k_last = (qb * PA_TQ + PA_TQ - 1) // PA_TK
    rows = B_HEADS * PA_TQ

    @pl.when(kb == 0)
    def _():
        m_ref[...] = jnp.full_like(m_ref, -jnp.inf)
        l_ref[...] = jnp.zeros_like(l_ref)
        acc_ref[...] = jnp.zeros_like(acc_ref)

    @pl.when(kb <= k_last)
    def _():
        ql = ql_ref[...].reshape(rows, B_KV_RANK).astype(BF16)
        qp = qp_ref[...].reshape(rows, B_ROPE).astype(BF16)
        ckv = ckv_ref[...].astype(BF16)
        kpe = kpe_ref[...].astype(BF16)
        dn = (((1,), (1,)), ((), ()))
        s = (lax.dot_general(ql, ckv, dn, preferred_element_type=F32)
             + lax.dot_general(qp, kpe, dn, preferred_element_type=F32)) * B_SCALE
        qpos = qb * PA_TQ + lax.broadcasted_iota(jnp.int32, (B_HEADS, PA_TQ, PA_TK), 1).reshape(rows, PA_TK)
        kpos = kb * PA_TK + lax.broadcasted_iota(jnp.int32, (rows, PA_TK), 1)
        s = jnp.where(kpos <= qpos, s, -jnp.inf)
        m_old = m_ref[...]
        m_new = jnp.maximum(m_old, jnp.max(s, axis=-1, keepdims=True))
        alpha = jnp.exp(m_old - m_new)
        p = jnp.exp(s - m_new)
        l_ref[...] = alpha * l_ref[...] + jnp.sum(p, axis=-1, keepdims=True)
        acc_ref[...] = alpha * acc_ref[...] + jnp.dot(p.astype(BF16), ckv, preferred_element_type=F32)
        m_ref[...] = m_new

    @pl.when(kb == k_last)
    def _():
        _uv_project(acc_ref[...], l_ref[...], wkv_ref, o_ref, PA_TQ)


def _prompt_attn(q_lat, q_pe, ckv, kpe, w_ukv_l):
    nqb = SEQ // PA_TQ
    nkb = SEQ // PA_TK
    rows = B_HEADS * PA_TQ

    def kv_idx(b, qb, kb):
        return (b * nkb + jnp.minimum(kb, (qb * PA_TQ + PA_TQ - 1) // PA_TK), 0)

    return pl.pallas_call(
        _pattn_kernel,
        grid=(BATCH, nqb, nkb),
        in_specs=[
            pl.BlockSpec((B_HEADS, PA_TQ, B_KV_RANK), lambda b, qb, kb: (0, b * nqb + qb, 0)),
            pl.BlockSpec((B_HEADS, PA_TQ, B_ROPE), lambda b, qb, kb: (0, b * nqb + qb, 0)),
            pl.BlockSpec((PA_TK, B_KV_RANK), kv_idx),
            pl.BlockSpec((PA_TK, B_ROPE), kv_idx),
            pl.BlockSpec((B_KV_RANK, B_HEADS * (B_NOPE + B_V)), lambda b, qb, kb: (0, 0)),
        ],
        out_specs=pl.BlockSpec((PA_TQ, B_HEADS * B_V), lambda b, qb, kb: (b * nqb + qb, 0)),
        out_shape=jax.ShapeDtypeStruct((M_P, B_HEADS * B_V), F32),
        scratch_shapes=[pltpu.VMEM((rows, 1), F32), pltpu.VMEM((rows, 1), F32), pltpu.VMEM((rows, B_KV_RANK), F32)],
        compiler_params=_cparams("arbitrary", "arbitrary", "arbitrary"),
        name="prompt_attn",
    )(q_lat, q_pe, ckv, kpe, w_ukv_l)


SA_PG = 32
SA_NCH = N_PAGES // SA_PG
SA_ROWS = B_HEADS * DEC_SEQ


def _sattn_kernel(pt_ref, *refs):
    ql_ref, qp_ref, cn_ref, kn_ref, wkv_ref = refs[:5]
    ckv_refs = refs[5:5 + SA_PG]
    kpe_refs = refs[5 + SA_PG:5 + 2 * SA_PG]
    o_ref, m_ref, l_ref, acc_ref, kc_ref, kp_ref = refs[5 + 2 * SA_PG:]
    ch = pl.program_id(1)
    dn = (((1,), (1,)), ((), ()))
    ql = ql_ref[...].reshape(SA_ROWS, B_KV_RANK).astype(BF16)
    qp = qp_ref[...].reshape(SA_ROWS, B_ROPE).astype(BF16)

    @pl.when(ch == 0)
    def _():
        m_ref[...] = jnp.full_like(m_ref, -jnp.inf)
        l_ref[...] = jnp.zeros_like(l_ref)
        acc_ref[...] = jnp.zeros_like(acc_ref)

    for j in range(SA_PG):
        kc_ref[j * PAGE_SIZE:(j + 1) * PAGE_SIZE, :] = ckv_refs[j][...].astype(BF16)
        kp_ref[:, j * PAGE_SIZE:(j + 1) * PAGE_SIZE] = kpe_refs[j][...].astype(BF16)
    kc = kc_ref[...]
    s = (lax.dot_general(ql, kc, dn, preferred_element_type=F32)
         + jnp.dot(qp, kp_ref[...], preferred_element_type=F32)) * B_SCALE
    m_old = m_ref[...]
    m_new = jnp.maximum(m_old, jnp.max(s, axis=-1, keepdims=True))
    alpha = jnp.exp(m_old - m_new)
    p = jnp.exp(s - m_new)
    l_ref[...] = alpha * l_ref[...] + jnp.sum(p, axis=-1, keepdims=True)
    acc_ref[...] = alpha * acc_ref[...] + jnp.dot(p.astype(BF16), kc, preferred_element_type=F32)
    m_ref[...] = m_new

    @pl.when(ch == pl.num_programs(1) - 1)
    def _():
        cn = cn_ref[...].astype(BF16)
        kn = kn_ref[...].astype(BF16)
        sn = (lax.dot_general(ql, cn, dn, preferred_element_type=F32)
              + lax.dot_general(qp, kn, dn, preferred_element_type=F32)) * B_SCALE
        t_q = lax.broadcasted_iota(jnp.int32, (B_HEADS, DEC_SEQ, DEC_SEQ), 1).reshape(SA_ROWS, DEC_SEQ)
        t_k = lax.broadcasted_iota(jnp.int32, (SA_ROWS, DEC_SEQ), 1)
        sn = jnp.where(t_k <= t_q, sn, -jnp.inf)
        m_o = m_ref[...]
        m_f = jnp.maximum(m_o, jnp.max(sn, axis=-1, keepdims=True))
        al = jnp.exp(m_o - m_f)
        pn = jnp.exp(sn - m_f)
        l_f = al * l_ref[...] + jnp.sum(pn, axis=-1, keepdims=True)
        acc_f = al * acc_ref[...] + jnp.dot(pn.astype(BF16), cn, preferred_element_type=F32)
        _uv_project(acc_f, l_f, wkv_ref, o_ref, DEC_SEQ)


def _sample_attn(page_table_flat, q_lat, q_pe, ckv, kpe, w_ukv_l, cache_ckv, cache_kpe, layer):
    n_pool = cache_ckv.shape[1]
    cc = cache_ckv.reshape(DEPTH * n_pool, PAGE_SIZE, B_KV_RANK)
    ck = jnp.swapaxes(cache_kpe, 2, 3).reshape(DEPTH * n_pool, B_ROPE, PAGE_SIZE)
    rb0 = M_P // DEC_SEQ

    def page_idx(j):
        return lambda b, c, pt: (layer * n_pool + pt[b * N_PAGES + c * SA_PG + j], 0, 0)

    in_specs = [
        pl.BlockSpec((B_HEADS, DEC_SEQ, B_KV_RANK), lambda b, c, pt: (0, rb0 + b, 0)),
        pl.BlockSpec((B_HEADS, DEC_SEQ, B_ROPE), lambda b, c, pt: (0, rb0 + b, 0)),
        pl.BlockSpec((DEC_SEQ, B_KV_RANK), lambda b, c, pt: (rb0 + b, 0)),
        pl.BlockSpec((DEC_SEQ, B_ROPE), lambda b, c, pt: (rb0 + b, 0)),
        pl.BlockSpec((B_KV_RANK, B_HEADS * (B_NOPE + B_V)), lambda b, c, pt: (0, 0)),
    ]
    in_specs += [pl.BlockSpec((None, PAGE_SIZE, B_KV_RANK), page_idx(j)) for j in range(SA_PG)]
    in_specs += [pl.BlockSpec((None, B_ROPE, PAGE_SIZE), page_idx(j)) for j in range(SA_PG)]
    grid_spec = pltpu.PrefetchScalarGridSpec(
        num_scalar_prefetch=1,
        grid=(DEC_BATCH, SA_NCH),
        in_specs=in_specs,
        out_specs=pl.BlockSpec((DEC_SEQ, B_HEADS * B_V), lambda b, c, pt: (b, 0)),
        scratch_shapes=[pltpu.VMEM((SA_ROWS, 1), F32), pltpu.VMEM((SA_ROWS, 1), F32),
                        pltpu.VMEM((SA_ROWS, B_KV_RANK), F32),
                        pltpu.VMEM((SA_PG * PAGE_SIZE, B_KV_RANK), BF16), pltpu.VMEM((B_ROPE, SA_PG * PAGE_SIZE), BF16)],
    )
    return pl.pallas_call(
        _sattn_kernel,
        grid_spec=grid_spec,
        out_shape=jax.ShapeDtypeStruct((M_S, B_HEADS * B_V), F32),
        compiler_params=_cparams("arbitrary", "arbitrary"),
        name="sample_attn",
    )(page_table_flat, q_lat, q_pe, ckv, kpe, w_ukv_l, *([cc] * SA_PG), *([ck] * SA_PG))


def _cmlp_kernel(u_ref, v_ref, nw_ref, wp_ref, ws_ref, bp_ref, bs_ref, o_ref, vn_ref):
    i = pl.program_id(0)
    u = _gelu(u_ref[...])
    v = _gelu(v_ref[...])
    vn = v * lax.rsqrt(jnp.mean(v * v, axis=-1, keepdims=True) + EPS) * nw_ref[...]
    vn_ref[...] = vn
    vb = vn.astype(BF16)

    def run(w_ref, b_ref):
        for g in range(C_GROUPS):
            cols = slice(g * C_GROUP_DIM, (g + 1) * C_GROUP_DIM)
            mixed = jnp.dot(w_ref[g], vb[:, cols], preferred_element_type=F32) + b_ref[g]
            o_ref[:, cols] = u[:, cols] * mixed

    @pl.when(i < NT_P)
    def _():
        run(wp_ref, bp_ref)

    @pl.when(i >= NT_P)
    def _():
        run(ws_ref, bs_ref)


def _block_diag_ws(w_s_l, b_s_l, length):
    reps = TM // length
    r_i = lax.broadcasted_iota(jnp.int32, (TM, TM), 0)
    c_i = lax.broadcasted_iota(jnp.int32, (TM, TM), 1)
    keep = jnp.logical_and(r_i // length == c_i // length, r_i >= c_i)
    wbd = jnp.where(keep[None], jnp.tile(w_s_l[:, :length, :length], (1, reps, reps)), 0.0).astype(BF16)
    bcol = jnp.tile(b_s_l[:, :length], (1, reps)).reshape(C_GROUPS, TM, 1)
    return wbd, bcol


def _chunk_mlp(z, c_vnorm_l, w_s_l, b_s_l):
    wp, bp = _block_diag_ws(w_s_l, b_s_l, C_CHUNK)
    ws, bs = _block_diag_ws(w_s_l, b_s_l, DEC_SEQ)
    wspec = pl.BlockSpec((C_GROUPS, TM, TM), lambda i: (0, 0, 0))
    bspec = pl.BlockSpec((C_GROUPS, TM, 1), lambda i: (0, 0, 0))
    row = pl.BlockSpec((TM, C_WIDTH), lambda i: (i, 0))
    return pl.pallas_call(
        _cmlp_kernel,
        grid=(NT,),
        in_specs=[
            pl.BlockSpec((TM, C_WIDTH), lambda i: (i, Z_U // C_WIDTH)),
            pl.BlockSpec((TM, C_WIDTH), lambda i: (i, Z_V // C_WIDTH)),
            pl.BlockSpec((1, C_WIDTH), lambda i: (0, 0)),
            wspec, wspec, bspec, bspec,
        ],
        out_specs=[row, row],
        out_shape=[jax.ShapeDtypeStruct((M_ALL, C_WIDTH), F32), jax.ShapeDtypeStruct((M_ALL, C_WIDTH), F32)],
        compiler_params=_cparams("arbitrary"),
        name="chunk_mlp",
    )(z, z, c_vnorm_l.reshape(1, -1), wp, ws, bp, bs)


def _prep_w_in_tail(w):
    o = Z1_WIDTH
    segs = {}
    for name, width in (("a", A_HEADS), ("bt", A_HEADS), ("cq", B_Q_RANK), ("ckv", B_KV_RANK), ("kpe", B_ROPE),
                        ("u", C_WIDTH), ("v", C_WIDTH)):
        segs[name] = w[:, o:o + width]
        o += width
    assert o == w.shape[1]
    zeros = lambda n: jnp.zeros((w.shape[0], n), w.dtype)
    parts = [segs["u"], segs["v"], segs["ckv"], segs["cq"],
             segs["kpe"], zeros(LANES - B_ROPE), segs["a"], segs["bt"], zeros(LANES - 2 * A_HEADS)]
    out = jnp.concatenate(parts, axis=1).astype(BF16)
    assert out.shape[1] == Z_WIDTH
    return out


def _prep_w_uq(w):
    w3 = w.reshape(B_Q_RANK, B_HEADS, B_NOPE + B_ROPE)
    return jnp.concatenate([w3[:, :, :B_NOPE].reshape(B_Q_RANK, NOPE_W), w3[:, :, B_NOPE:].reshape(B_Q_RANK, ROPE_W)],
                           axis=1)


def _rope_tables():
    half = B_ROPE // 2
    inv = 1.0 / (ROPE_THETA ** (jnp.arange(half, dtype=F32) / half))
    pos = jnp.concatenate([jnp.tile(jnp.arange(SEQ, dtype=jnp.int32), BATCH),
                           jnp.tile(PAST_LEN + jnp.arange(DEC_SEQ, dtype=jnp.int32), DEC_BATCH)])
    ang = pos.astype(F32)[:, None] * inv[None, :]
    cos, sin = jnp.cos(ang), jnp.sin(ang)
    cos_t = jnp.tile(jnp.concatenate([cos, cos], axis=1), (1, B_HEADS))
    sin_t = jnp.tile(jnp.concatenate([-sin, sin], axis=1), (1, B_HEADS))
    return cos_t, sin_t


def _lane_row(vec):
    return jnp.zeros((1, LANES), F32).at[0, :vec.shape[0]].set(vec.astype(F32))


def kernel(x_prompt, x_sample, cache_ckv, cache_kpe, state_gdn, state_conv, page_table, c_prompt, c_sample, w_ada, b_ada, w_in, conv_w, a_log, dt_bias, gdn_norm, q_norm, kv_norm, w_uq, w_ukv, c_vnorm, w_s, b_s, w_out, w_gate, w_up, w_down, w_router, e_gate, e_up, e_down, final_norm):
    x = jnp.concatenate([x_prompt.reshape(M_P, D_MODEL), x_sample.reshape(M_S, D_MODEL)], axis=0)
    c_all = jnp.concatenate([c_prompt, c_sample, jnp.zeros((N_COND_PAD - N_COND, D_MODEL), F32)], axis=0)
    mod = _ada(c_all, w_ada, b_ada)
    cos_t, sin_t = _rope_tables()
    pt_flat = page_table.reshape(-1)

    def mods(l):
        return (mod[l, :BATCH].reshape(BATCH, 6, 1, D_MODEL), mod[l, BATCH:N_COND].reshape(DEC_BATCH, 6, 1, D_MODEL))

    outs = {k: [] for k in ("p_ckv", "p_kpe", "p_gdn", "p_conv", "s_ckv", "s_kpe", "s_gdn", "s_conv", "s_cv")}
    mp, ms = mods(0)
    h = _normmod(x, mp, ms, 0, 1)
    y = None
    for l in range(DEPTH):
        z1 = _mm(h, jnp.swapaxes(w_in, 1, 2), l, tn=512, name="in_proj_a", n_cols=Z1_WIDTH, w_t=True)
        z2 = _mm(h, _prep_w_in_tail(w_in[l]), 0, tn=Z_WIDTH // 2, name="in_proj_b")
        st_pad = jnp.concatenate([jnp.zeros((DEC_BATCH, SUBLANES - (A_CONV - 1), A_CONV_CH), F32), state_conv[l]],
                                 axis=1).reshape(M_S, A_CONV_CH)
        qkv = _conv_qkv(z1, st_pad, conv_w[l])
        gate_rows = (_lane_row(a_log[l]), _lane_row(dt_bias[l]), gdn_norm[l].reshape(1, A_DV).astype(F32))
        oa_p, sg_p = _gdn(qkv, z1, z2, gate_rows, None, row0=0, nseq=BATCH, seq_len=SEQ, c=A_CHUNK, hb=GDN_HB_PROMPT,
                          name="gdn_prompt")
        oa_s, sg_s = _gdn(qkv, z1, z2, gate_rows, state_gdn.reshape((-1,) + state_gdn.shape[2:]), row0=M_P,
                          nseq=DEC_BATCH, seq_len=DEC_SEQ, c=DEC_SEQ, hb=A_HEADS, name="gdn_sample",
                          s0_seq0=l * DEC_BATCH)
        q_lat, q_pe, ckv, kpe = _mla_proj(z2, cos_t, sin_t, q_norm[l], kv_norm[l], _prep_w_uq(w_uq[l]), w_ukv[l])
        ob_p = _prompt_attn(q_lat, q_pe, ckv, kpe, w_ukv[l])
        ob_s = _sample_attn(pt_flat, q_lat, q_pe, ckv, kpe, w_ukv[l], cache_ckv, cache_kpe, l)
        o_c, vn = _chunk_mlp(z2, c_vnorm[l], w_s[l], b_s[l])
        cat = jnp.concatenate([jnp.concatenate([oa_p, oa_s], axis=0), jnp.concatenate([ob_p, ob_s], axis=0), o_c],
                              axis=1).astype(BF16)
        f = _mm(cat, w_out, l, tn=512, name="out_proj")
        x, h = _resid(x, [f], None, mp, ms, 2, nxt=(mp, ms, 3, 4))

        outs["p_ckv"].append(ckv[:M_P].reshape(BATCH, SEQ, B_KV_RANK))
        outs["p_kpe"].append(kpe[:M_P].reshape(BATCH, SEQ, B_ROPE))
        outs["p_gdn"].append(sg_p)
        outs["p_conv"].append(jnp.stack([z1[(b + 1) * SEQ - (A_CONV - 1):(b + 1) * SEQ, :A_CONV_CH]
                                         for b in range(BATCH)]))
        outs["s_ckv"].append(ckv[M_P:].reshape(DEC_BATCH, DEC_SEQ, B_KV_RANK))
        outs["s_kpe"].append(kpe[M_P:].reshape(DEC_BATCH, DEC_SEQ, B_ROPE))
        outs["s_gdn"].append(sg_s)
        outs["s_conv"].append(z1[M_P:, :A_CONV_CH].reshape(DEC_BATCH, DEC_SEQ, A_CONV_CH)[:, DEC_SEQ - (A_CONV - 1):])
        outs["s_cv"].append(vn[M_P:].reshape(DEC_BATCH, DEC_SEQ, C_WIDTH))

        m_idx = l // 2
        if l % 2 == 0:
            hid = _swiglu_up(h, w_gate.reshape((-1, 1) + w_gate.shape[1:]), w_up.reshape((-1, 1) + w_up.shape[1:]),
                             m_idx, _dense_plan(M_ALL // UP_TM))
            fs = [_swiglu_down(hid, w_down.reshape((-1, 1) + w_down.shape[1:]), m_idx, _dense_plan(M_ALL // DOWN_TM))]
            tw = None
        else:
            w_r = jnp.zeros((D_MODEL, LANES), BF16).at[:, :N_EXPERTS].set(w_router[m_idx].astype(BF16))
            ti, tw = _router(h, w_r)
            src, pos, plan_up, plan_down = _route_plan(ti[:, :TOP_K])
            xs = jnp.take(h, src, axis=0, mode="clip")
            hid = _swiglu_up(xs, e_gate, e_up, m_idx, plan_up)
            ys = _swiglu_down(hid, e_down, m_idx, plan_down)
            fs = [jnp.take(ys, pos[:, k], axis=0, mode="clip") for k in range(TOP_K)]
        if l + 1 < DEPTH:
            mp_n, ms_n = mods(l + 1)
            x, h = _resid(x, fs, tw, mp, ms, 5, nxt=(mp_n, ms_n, 0, 1))
            mp, ms = mp_n, ms_n
        else:
            y_p, y_s = _resid(x, fs, tw, mp, ms, 5, final_w=final_norm)

    st = lambda k: jnp.stack(outs[k])
    return (y_p.reshape(BATCH, SEQ, D_MODEL), y_s.reshape(DEC_BATCH, DEC_SEQ, D_MODEL),
            st("p_ckv"), st("p_kpe"), st("p_gdn"), st("p_conv"),
            st("s_ckv"), st("s_kpe"), st("s_gdn"), st("s_conv"), st("s_cv"))
```
